```python
import math
import jax, jax.numpy as jnp
from jax import lax
import numpy as np


D_MODEL = 4096
BATCH = 1
SEQ = 8192
DEPTH = 1
DEC_BATCH = 16
DEC_SEQ = 16
PAST_LEN = 2048

CHUNK = 64
QBLK = 128
HEAD_DIM = 128
MIX_WIDTH = D_MODEL
A_WIDTH = MIX_WIDTH // 2
B_WIDTH = MIX_WIDTH - A_WIDTH
A_HEADS = A_WIDTH // (2 * HEAD_DIM)
B_HEADS = B_WIDTH // HEAD_DIM
ROT_DIM = HEAD_DIM // 4
ROPE_THETA = 500000.0
D_FF = 4 * D_MODEL
IN_WIDTH = 3 * A_WIDTH + 3 * B_WIDTH
EPS = 1e-6
NEG = -1e30

kernel_name = "hybrid_diff_stickbreaking_stream_step"


def _rmsnorm(x, g):
    xf = x.astype(jnp.float32)
    y = xf * lax.rsqrt(jnp.mean(xf * xf, axis=-1, keepdims=True) + EPS)
    return (y * g.astype(jnp.float32)).astype(x.dtype)


def _partial_rope(x, pos):
    half = ROT_DIM // 2
    inv = 1.0 / (ROPE_THETA ** (jnp.arange(0, ROT_DIM, 2, dtype=jnp.float32) / ROT_DIM))
    ang = pos.astype(jnp.float32)[:, None] * inv[None, :]
    shape = (1, ang.shape[0]) + (1,) * (x.ndim - 3) + (half,)
    cos = jnp.cos(ang).reshape(shape).astype(x.dtype)
    sin = jnp.sin(ang).reshape(shape).astype(x.dtype)
    x1 = x[..., :half]
    x2 = x[..., half:ROT_DIM]
    return jnp.concatenate([x1 * cos - x2 * sin, x2 * cos + x1 * sin, x[..., ROT_DIM:]], axis=-1)


def _project(h, w_in, pos):
    B, T, _ = h.shape
    proj = jnp.einsum('btd,de->bte', h, w_in)
    aq, ak, av, bq, bk, bv = jnp.split(
        proj, [A_WIDTH, 2 * A_WIDTH, 3 * A_WIDTH, 3 * A_WIDTH + B_WIDTH, 3 * A_WIDTH + 2 * B_WIDTH], axis=-1)
    aq = _partial_rope(aq.reshape(B, T, A_HEADS, 2, HEAD_DIM), pos)
    ak = _partial_rope(ak.reshape(B, T, A_HEADS, 2, HEAD_DIM), pos)
    av = av.reshape(B, T, A_HEADS, 2 * HEAD_DIM)
    bq = bq.reshape(B, T, B_HEADS, HEAD_DIM)
    bk = bk.reshape(B, T, B_HEADS, HEAD_DIM)
    bv = bv.reshape(B, T, B_HEADS, HEAD_DIM)
    return aq, ak, av, bq, bk, bv


def _diff_attn_block(q, q_pos, k, v, k_pos, lam):
    s = jnp.einsum('bqhcd,bkhcd->bhcqk', q, k).astype(jnp.float32) * (HEAD_DIM ** -0.5)
    mask = (k_pos[None, :] // CHUNK) <= (q_pos[:, None] // CHUNK)
    s = jnp.where(mask, s, NEG)
    p = jax.nn.softmax(s, axis=-1)
    attn = p[:, :, 0] - lam * p[:, :, 1]
    return jnp.einsum('bhqk,bkhe->bqhe', attn.astype(v.dtype), v)


def _stick_block(q, q_pos, k, v, k_pos):
    z = jnp.einsum('bqhd,bkhd->bhqk', q, k).astype(jnp.float32) * (HEAD_DIM ** -0.5)
    mask = k_pos[None, :] < q_pos[:, None]
    log_beta = jax.nn.log_sigmoid(z)
    log_1mb = jnp.where(mask, jax.nn.log_sigmoid(-z), 0.0)
    tail = lax.cumsum(log_1mb, axis=3, reverse=True) - log_1mb
    a = jnp.where(mask, jnp.exp(log_beta + tail), 0.0)
    return jnp.einsum('bhqk,bkhd->bqhd', a.astype(v.dtype), v)


def _sweep(fn, q, q_pos, *rest):
    B, T = q.shape[0], q.shape[1]
    nb = T // QBLK
    qb = jnp.moveaxis(q.reshape((B, nb, QBLK) + q.shape[2:]), 1, 0)
    pb = q_pos.reshape(nb, QBLK)
    out = lax.map(lambda a: fn(a[0], a[1], *rest), (qb, pb))
    return jnp.moveaxis(out, 0, 1).reshape((B, T) + out.shape[3:])


def _merge(oa, ob, g_subln, g_sb, lam_init, w_out):
    B, T = oa.shape[0], oa.shape[1]
    oa = _rmsnorm(oa, g_subln) * (1.0 - lam_init)
    ob = _rmsnorm(ob, g_sb)
    o = jnp.concatenate([oa.reshape(B, T, A_WIDTH), ob.reshape(B, T, B_WIDTH)], axis=-1)
    return jnp.einsum('bte,ed->btd', o, w_out)


def _mlp(x, g, w_up, w_down):
    h = _rmsnorm(x, g)
    u = jnp.square(jax.nn.relu(jnp.einsum('btd,df->btf', h, w_up)))
    return jnp.einsum('btf,fd->btd', u, w_down)


def setup_inputs(seed: int = 0) -> dict:
    key = jax.random.key(seed)
    ks = jax.random.split(key, 20)
    f32 = jnp.float32
    nrm = lambda k, shape, s: jax.random.normal(k, shape, f32) * s
    return {
        'x_prompt': nrm(ks[0], (BATCH, SEQ, D_MODEL), 1.0),
        'x_sample': nrm(ks[1], (DEC_BATCH, DEC_SEQ, D_MODEL), 1.0),
        'cache_a_k': nrm(ks[2], (DEPTH, DEC_BATCH, PAST_LEN, A_HEADS, 2, HEAD_DIM), 1.0),
        'cache_a_v': nrm(ks[3], (DEPTH, DEC_BATCH, PAST_LEN, A_HEADS, 2 * HEAD_DIM), 1.0),
        'cache_b_k': nrm(ks[4], (DEPTH, DEC_BATCH, PAST_LEN, B_HEADS, HEAD_DIM), 1.0),
        'cache_b_v': nrm(ks[5], (DEPTH, DEC_BATCH, PAST_LEN, B_HEADS, HEAD_DIM), 1.0),
        'g_attn': 1.0 + nrm(ks[6], (DEPTH, D_MODEL), 0.02),
        'w_in': nrm(ks[7], (DEPTH, D_MODEL, IN_WIDTH), D_MODEL ** -0.5),
        'lambda_q1': nrm(ks[8], (DEPTH, HEAD_DIM), 0.1),
        'lambda_k1': nrm(ks[9], (DEPTH, HEAD_DIM), 0.1),
        'lambda_q2': nrm(ks[10], (DEPTH, HEAD_DIM), 0.1),
        'lambda_k2': nrm(ks[11], (DEPTH, HEAD_DIM), 0.1),
        'g_subln': 1.0 + nrm(ks[12], (DEPTH, 2 * HEAD_DIM), 0.02),
        'g_sb': 1.0 + nrm(ks[13], (DEPTH, HEAD_DIM), 0.02),
        'w_out': nrm(ks[14], (DEPTH, MIX_WIDTH, D_MODEL), MIX_WIDTH ** -0.5),
        'g_mlp': 1.0 + nrm(ks[15], (DEPTH, D_MODEL), 0.02),
        'w_up': nrm(ks[16], (DEPTH, D_MODEL, D_FF), D_MODEL ** -0.5),
        'w_down': nrm(ks[17], (DEPTH, D_FF, D_MODEL), D_FF ** -0.5),
        'g_final': 1.0 + nrm(ks[18], (D_MODEL,), 0.02),
    }


def reference(x_prompt, x_sample, cache_a_k, cache_a_v, cache_b_k, cache_b_v,
              g_attn, w_in, lambda_q1, lambda_k1, lambda_q2, lambda_k2,
              g_subln, g_sb, w_out, g_mlp, w_up, w_down, g_final):
    seq = x_prompt.shape[1]
    dec_seq = x_sample.shape[1]
    past = cache_a_k.shape[2]
    pos_p = jnp.arange(seq, dtype=jnp.int32)
    pos_s = past + jnp.arange(dec_seq, dtype=jnp.int32)
    pos_kv = jnp.arange(past + dec_seq, dtype=jnp.int32)

    xp, xs = x_prompt, x_sample
    akp, avp, bkp, bvp = [], [], [], []
    aks, avs, bks, bvs = [], [], [], []
    for l in range(DEPTH):
        lam_init = 0.8 - 0.6 * math.exp(-0.3 * l)
        lam = (jnp.exp(jnp.sum(lambda_q1[l].astype(jnp.float32) * lambda_k1[l].astype(jnp.float32)))
               - jnp.exp(jnp.sum(lambda_q2[l].astype(jnp.float32) * lambda_k2[l].astype(jnp.float32)))
               + lam_init)

        hp = _rmsnorm(xp, g_attn[l])
        aq, ak, av, bq, bk, bv = _project(hp, w_in[l], pos_p)
        oa = _sweep(_diff_attn_block, aq, pos_p, ak, av, pos_p, lam)
        ob = _sweep(_stick_block, bq, pos_p, bk, bv, pos_p)
        xp = xp + _merge(oa, ob, g_subln[l], g_sb[l], lam_init, w_out[l])
        xp = xp + _mlp(xp, g_mlp[l], w_up[l], w_down[l])
        akp.append(ak); avp.append(av); bkp.append(bk); bvp.append(bv)

        hs = _rmsnorm(xs, g_attn[l])
        aq, ak, av, bq, bk, bv = _project(hs, w_in[l], pos_s)
        ka = jnp.concatenate([cache_a_k[l], ak], axis=1)
        va = jnp.concatenate([cache_a_v[l], av], axis=1)
        kb = jnp.concatenate([cache_b_k[l], bk], axis=1)
        vb = jnp.concatenate([cache_b_v[l], bv], axis=1)
        oa = _diff_attn_block(aq, pos_s, ka, va, pos_kv, lam)
        ob = _stick_block(bq, pos_s, kb, vb, pos_kv)
        xs = xs + _merge(oa, ob, g_subln[l], g_sb[l], lam_init, w_out[l])
        xs = xs + _mlp(xs, g_mlp[l], w_up[l], w_down[l])
        aks.append(ak); avs.append(av); bks.append(bk); bvs.append(bv)

    y_prompt = _rmsnorm(xp, g_final)
    y_sample = _rmsnorm(xs, g_final)
    return (y_prompt, y_sample,
            jnp.stack(akp), jnp.stack(avp), jnp.stack(bkp), jnp.stack(bvp),
            jnp.stack(aks), jnp.stack(avs), jnp.stack(bks), jnp.stack(bvs))
```

```python
import functools
import math

import jax
import jax.numpy as jnp
from jax import lax
from jax.experimental import pallas as pl
from jax.experimental.pallas import tpu as pltpu

D_MODEL = 4096
HEAD_DIM = 128
A_HEADS = 8
B_HEADS = 16
A_WIDTH = A_HEADS * 2 * HEAD_DIM
B_WIDTH = B_HEADS * HEAD_DIM
ROT_DIM = HEAD_DIM // 4
ROPE_THETA = 500000.0
CHUNK = 64
D_FF = 4 * D_MODEL
EPS = 1e-6
NEG = -1e30
SCALE = HEAD_DIM ** -0.5

VMEM_LIMIT_BYTES = 56 * 1024 * 1024

F32 = jnp.float32
BF16 = jnp.bfloat16


def _params(*sem):
    return pltpu.CompilerParams(dimension_semantics=sem, vmem_limit_bytes=VMEM_LIMIT_BYTES)


def _rmsnorm_kernel(x_ref, g_ref, o_ref):
    x = x_ref[...]
    ms = jnp.mean(x * x, axis=-1, keepdims=True)
    o_ref[...] = (x * lax.rsqrt(ms + EPS) * g_ref[...]).astype(o_ref.dtype)


def _rmsnorm(x, g, out_dtype, tm=256):
    m, d = x.shape
    return pl.pallas_call(
        _rmsnorm_kernel,
        grid=(m // tm,),
        in_specs=[pl.BlockSpec((tm, d), lambda i: (i, 0)), pl.BlockSpec((1, d), lambda i: (0, 0))],
        out_specs=pl.BlockSpec((tm, d), lambda i: (i, 0)),
        out_shape=jax.ShapeDtypeStruct((m, d), out_dtype),
        compiler_params=_params("arbitrary"),
        name="rmsnorm",
    )(x, g.reshape(1, d))


def _rope_table_kernel(inv_ref, c_ref, s1_ref, s2_ref, *, tm, base, period):
    half = ROT_DIM // 2
    row = lax.broadcasted_iota(jnp.int32, (tm, HEAD_DIM), 0) + pl.program_id(0) * tm
    lane = lax.broadcasted_iota(jnp.int32, (tm, HEAD_DIM), 1)
    pos = (base + row % period).astype(F32)
    ang = pos * inv_ref[...]
    cos = jnp.cos(ang)
    sin = jnp.sin(ang)
    c_ref[...] = jnp.where(lane < ROT_DIM, cos, 1.0)
    s1_ref[...] = jnp.where(lane < half, -sin, 0.0)
    s2_ref[...] = jnp.where((lane >= half) & (lane < ROT_DIM), sin, 0.0)


def _rope_tables(m, base, period, tm=256):
    half = ROT_DIM // 2
    inv = 1.0 / (ROPE_THETA ** (jnp.arange(0, ROT_DIM, 2, dtype=F32) / ROT_DIM))
    inv_lane = jnp.concatenate([inv, inv, jnp.zeros((HEAD_DIM - ROT_DIM,), F32)]).reshape(1, HEAD_DIM)
    assert inv.shape[0] == half
    spec = pl.BlockSpec((tm, HEAD_DIM), lambda i: (i, 0))
    shp = jax.ShapeDtypeStruct((m, HEAD_DIM), F32)
    return pl.pallas_call(
        functools.partial(_rope_table_kernel, tm=tm, base=base, period=period),
        grid=(m // tm,),
        in_specs=[pl.BlockSpec((1, HEAD_DIM), lambda i: (0, 0))],
        out_specs=[spec, spec, spec],
        out_shape=[shp, shp, shp],
        compiler_params=_params("arbitrary"),
        name="rope_tables",
    )(inv_lane)


def _mm_accumulate(x_ref, w_ref, acc_ref):
    @pl.when(pl.program_id(2) == 0)
    def _():
        acc_ref[...] = jnp.zeros_like(acc_ref)

    acc_ref[...] += jnp.dot(x_ref[...], w_ref[...], preferred_element_type=F32)


def _inproj_kernel(x_ref, w_ref, c_ref, s1_ref, s2_ref, *rest, nk, rope, scale, want_f32, want_bf16):
    outs, acc_ref = rest[:-1], rest[-1]
    _mm_accumulate(x_ref, w_ref, acc_ref)

    @pl.when(pl.program_id(2) == nk - 1)
    def _():
        tn = acc_ref.shape[1]
        for hh in range(tn // HEAD_DIM):
            sl = slice(hh * HEAD_DIM, (hh + 1) * HEAD_DIM)
            y = acc_ref[:, sl]
            if rope:
                half = ROT_DIM // 2
                y = (y * c_ref[...]
                     + pltpu.roll(y, HEAD_DIM - half, 1) * s1_ref[...]
                     + pltpu.roll(y, half, 1) * s2_ref[...])
            k = 0
            if want_f32:
                outs[k][:, sl] = y
                k += 1
            if want_bf16:
                outs[k][:, sl] = (y * scale).astype(BF16)


def _inproj(h, w, tabs, col_block, n, *, rope, scale, want_f32, want_bf16, tm, tn=1024, tk=1024):
    m, kdim = h.shape
    nk = kdim // tk
    off = col_block * (n // tn)
    tab_spec = pl.BlockSpec((tm, HEAD_DIM), lambda i, j, k: (i, 0))
    out_specs, out_shape = [], []
    if want_f32:
        out_specs.append(pl.BlockSpec((tm, tn), lambda i, j, k: (i, j)))
        out_shape.append(jax.ShapeDtypeStruct((m, n), F32))
    if want_bf16:
        out_specs.append(pl.BlockSpec((tm, tn), lambda i, j, k: (i, j)))
        out_shape.append(jax.ShapeDtypeStruct((m, n), BF16))
    return pl.pallas_call(
        functools.partial(_inproj_kernel, nk=nk, rope=rope, scale=scale, want_f32=want_f32, want_bf16=want_bf16),
        grid=(m // tm, n // tn, nk),
        in_specs=[pl.BlockSpec((tm, tk), lambda i, j, k: (i, k)),
                  pl.BlockSpec((tk, tn), lambda i, j, k: (k, j + off)),
                  tab_spec, tab_spec, tab_spec],
        out_specs=out_specs,
        out_shape=out_shape,
        scratch_shapes=[pltpu.VMEM((tm, tn), F32)],
        compiler_params=_params("parallel", "parallel", "arbitrary"),
        name="inproj",
    )(h, w, *tabs)


def _mm_residual_kernel(x_ref, w_ref, r_ref, o_ref, acc_ref, *, nk):
    _mm_accumulate(x_ref, w_ref, acc_ref)

    @pl.when(pl.program_id(2) == nk - 1)
    def _():
        o_ref[...] = r_ref[...] + acc_ref[...]


def _mm_residual(x, w, r, *, tm, tn=1024, tk=1024):
    m, kdim = x.shape
    n = w.shape[1]
    nk = kdim // tk
    return pl.pallas_call(
        functools.partial(_mm_residual_kernel, nk=nk),
        grid=(m // tm, n // tn, nk),
        in_specs=[pl.BlockSpec((tm, tk), lambda i, j, k: (i, k)),
                  pl.BlockSpec((tk, tn), lambda i, j, k: (k, j)),
                  pl.BlockSpec((tm, tn), lambda i, j, k: (i, j))],
        out_specs=pl.BlockSpec((tm, tn), lambda i, j, k: (i, j)),
        out_shape=jax.ShapeDtypeStruct((m, n), F32),
        scratch_shapes=[pltpu.VMEM((tm, tn), F32)],
        compiler_params=_params("parallel", "parallel", "arbitrary"),
        name="mm_residual",
    )(x, w, r)


def _mm_relu2_kernel(x_ref, w_ref, o_ref, acc_ref, *, nk):
    _mm_accumulate(x_ref, w_ref, acc_ref)

    @pl.when(pl.program_id(2) == nk - 1)
    def _():
        u = jnp.maximum(acc_ref[...], 0.0)
        o_ref[...] = (u * u).astype(o_ref.dtype)


def _mm_relu2(x, w, *, tm, tn=1024, tk=1024):
    m, kdim = x.shape
    n = w.shape[1]
    nk = kdim // tk
    return pl.pallas_call(
        functools.partial(_mm_relu2_kernel, nk=nk),
        grid=(m // tm, n // tn, nk),
        in_specs=[pl.BlockSpec((tm, tk), lambda i, j, k: (i, k)),
                  pl.BlockSpec((tk, tn), lambda i, j, k: (k, j))],
        out_specs=pl.BlockSpec((tm, tn), lambda i, j, k: (i, j)),
        out_shape=jax.ShapeDtypeStruct((m, n), BF16),
        scratch_shapes=[pltpu.VMEM((tm, tn), F32)],
        compiler_params=_params("parallel", "parallel", "arbitrary"),
        name="mm_relu2",
    )(x, w)


def _lambda(lam_ref, lam_init):
    lp = lam_ref[...]
    s1 = jnp.sum(lp[0:1] * lp[1:2], axis=-1, keepdims=True)
    s2 = jnp.sum(lp[2:3] * lp[3:4], axis=-1, keepdims=True)
    return jnp.exp(s1) - jnp.exp(s2) + lam_init


def _head_rmsnorm(o, g, mult=None):
    y = o * lax.rsqrt(jnp.mean(o * o, axis=-1, keepdims=True) + EPS) * g
    if mult is not None:
        y = y * mult
    return y


def _dot_nt(a, b):
    return lax.dot_general(a, b, (((1,), (1,)), ((), ())), preferred_element_type=F32)


def _log_sigmoid(z):
    return jnp.minimum(z, 0.0) - jnp.log(1.0 + jnp.exp(-jnp.abs(z)))


def _strict_upper(n):
    r = lax.broadcasted_iota(jnp.int32, (n, n), 0)
    c = lax.broadcasted_iota(jnp.int32, (n, n), 1)
    return (r > c).astype(BF16)


def _diff_prompt_kernel(q_ref, k_ref, v_ref, lam_ref, g_ref, o_ref, m_sc, l_sc, acc_sc, *, tq, tk, lam_init):
    qi = pl.program_id(1)
    q = q_ref[...]
    m_sc[...] = jnp.full_like(m_sc, NEG)
    l_sc[...] = jnp.zeros_like(l_sc)
    acc_sc[...] = jnp.zeros_like(acc_sc)

    def block(kstart, masked):
        kb = k_ref[pl.ds(kstart, tk), :]
        vb = v_ref[pl.ds(kstart, tk), :]
        for c in range(2):
            sl = slice(c * HEAD_DIM, (c + 1) * HEAD_DIM)
            s = _dot_nt(q[:, sl], kb[:, sl])
            if masked:
                qc = (lax.broadcasted_iota(jnp.int32, (tq, tk), 0)) // CHUNK
                kc = (lax.broadcasted_iota(jnp.int32, (tq, tk), 1)) // CHUNK
                s = jnp.where(kc <= qc, s, NEG)
            m_old = m_sc[c]
            m_new = jnp.maximum(m_old, jnp.max(s, axis=-1, keepdims=True))
            alpha = jnp.exp(m_old - m_new)
            p = jnp.exp(s - m_new)
            l_sc[c] = alpha * l_sc[c] + jnp.sum(p, axis=-1, keepdims=True)
            acc_sc[c] = alpha * acc_sc[c] + jnp.dot(p.astype(BF16), vb, preferred_element_type=F32)
            m_sc[c] = m_new

    def body(j, carry):
        block(pl.multiple_of(j * tk, tk), False)
        return carry

    lax.fori_loop(0, qi * (tq // tk), body, 0)
    block(pl.multiple_of(qi * tq, tq), True)

    lam = _lambda(lam_ref, lam_init)
    o = acc_sc[0] / l_sc[0] - lam * (acc_sc[1] / l_sc[1])
    o_ref[...] = _head_rmsnorm(o, g_ref[...], 1.0 - lam_init).astype(o_ref.dtype)


def _diff_prompt(q, k, v, lam_p, g, lam_init, tq=256):
    t = q.shape[0]
    tk = tq
    hw = 2 * HEAD_DIM
    return pl.pallas_call(
        functools.partial(_diff_prompt_kernel, tq=tq, tk=tk, lam_init=lam_init),
        grid=(A_HEADS, t // tq),
        in_specs=[pl.BlockSpec((tq, hw), lambda h, i: (i, h)),
                  pl.BlockSpec((t, hw), lambda h, i: (0, h)),
                  pl.BlockSpec((t, hw), lambda h, i: (0, h)),
                  pl.BlockSpec((4, HEAD_DIM), lambda h, i: (0, 0)),
                  pl.BlockSpec((1, hw), lambda h, i: (0, 0))],
        out_specs=pl.BlockSpec((tq, hw), lambda h, i: (i, h)),
        out_shape=jax.ShapeDtypeStruct((t, A_WIDTH), BF16),
        scratch_shapes=[pltpu.VMEM((2, tq, 1), F32), pltpu.VMEM((2, tq, 1), F32), pltpu.VMEM((2, tq, hw), F32)],
        compiler_params=_params("parallel", "arbitrary"),
        name="diff_prompt",
    )(q, k, v, lam_p, g)


def _stick_prompt_kernel(q_ref, k_ref, v_ref, g_ref, o_ref, carry_sc, acc_sc, *, tq):
    qi = pl.program_id(1)
    q = q_ref[...]
    upper = _strict_upper(tq)

    def block(kstart, masked):
        kb = k_ref[pl.ds(kstart, tq), :]
        vb = v_ref[pl.ds(kstart, tq), :]
        z = _dot_nt(q, kb)
        lb = _log_sigmoid(z)
        l1 = lb - z
        if masked:
            mask = (lax.broadcasted_iota(jnp.int32, (tq, tq), 1)
                    < lax.broadcasted_iota(jnp.int32, (tq, tq), 0))
            l1 = jnp.where(mask, l1, 0.0)
        tail = jnp.dot(l1.astype(BF16), upper, preferred_element_type=F32) + carry_sc[...]
        a = jnp.exp(lb + tail)
        if masked:
            a = jnp.where(mask, a, 0.0)
        carry_sc[...] += jnp.sum(l1, axis=-1, keepdims=True)
        acc_sc[...] += jnp.dot(a.astype(BF16), vb, preferred_element_type=F32)

    carry_sc[...] = jnp.zeros_like(carry_sc)
    acc_sc[...] = jnp.zeros_like(acc_sc)
    block(pl.multiple_of(qi * tq, tq), True)

    def body(j, c):
        block(pl.multiple_of((qi - 1 - j) * tq, tq), False)
        return c

    lax.fori_loop(0, qi, body, 0)
    o_ref[...] = _head_rmsnorm(acc_sc[...], g_ref[...]).astype(o_ref.dtype)


def _stick_prompt(q, k, v, g, tq=256):
    t = q.shape[0]
    return pl.pallas_call(
        functools.partial(_stick_prompt_kernel, tq=tq),
        grid=(B_HEADS, t // tq),
        in_specs=[pl.BlockSpec((tq, HEAD_DIM), lambda h, i: (i, h)),
                  pl.BlockSpec((t, HEAD_DIM), lambda h, i: (0, h)),
                  pl.BlockSpec((t, HEAD_DIM), lambda h, i: (0, h)),
                  pl.BlockSpec((1, HEAD_DIM), lambda h, i: (0, 0))],
        out_specs=pl.BlockSpec((tq, HEAD_DIM), lambda h, i: (i, h)),
        out_shape=jax.ShapeDtypeStruct((t, B_WIDTH), BF16),
        scratch_shapes=[pltpu.VMEM((tq, 1), F32), pltpu.VMEM((tq, HEAD_DIM), F32)],
        compiler_params=_params("parallel", "arbitrary"),
        name="stick_prompt",
    )(q, k, v, g)


def _diff_sample_kernel(q_ref, kc_ref, vc_ref, kn_ref, vn_ref, lam_ref, g_ref, o_ref, *, past, nq, lam_init):
    q = q_ref[...]
    kc = kc_ref[0].astype(BF16)
    vc = vc_ref[0].astype(BF16)
    kn = kn_ref[...]
    vn = vn_ref[...]
    qpos = past + lax.broadcasted_iota(jnp.int32, (nq, 1), 0)
    outs = []
    for c in range(2):
        sl = slice(c * HEAD_DIM, (c + 1) * HEAD_DIM)
        sc = _dot_nt(q[:, sl], kc[:, sl])
        sn = _dot_nt(q[:, sl], kn[:, sl])
        kpos_c = lax.broadcasted_iota(jnp.int32, (nq, past), 1)
        kpos_n = past + lax.broadcasted_iota(jnp.int32, (nq, nq), 1)
        sc = jnp.where(kpos_c // CHUNK <= qpos // CHUNK, sc, NEG)
        sn = jnp.where(kpos_n // CHUNK <= qpos // CHUNK, sn, NEG)
        m = jnp.maximum(jnp.max(sc, axis=-1, keepdims=True), jnp.max(sn, axis=-1, keepdims=True))
        pc = jnp.exp(sc - m)
        pn = jnp.exp(sn - m)
        l = jnp.sum(pc, axis=-1, keepdims=True) + jnp.sum(pn, axis=-1, keepdims=True)
        acc = (jnp.dot(pc.astype(BF16), vc, preferred_element_type=F32)
               + jnp.dot(pn.astype(BF16), vn, preferred_element_type=F32))
        outs.append(acc / l)
    lam = _lambda(lam_ref, lam_init)
    o = outs[0] - lam * outs[1]
    o_ref[...] = _head_rmsnorm(o, g_ref[...], 1.0 - lam_init).astype(o_ref.dtype)


def _diff_sample(q, kcache, vcache, knew, vnew, lam_p, g, lam_init, nq):
    nb, past, _ = kcache.shape
    hw = 2 * HEAD_DIM
    return pl.pallas_call(
        functools.partial(_diff_sample_kernel, past=past, nq=nq, lam_init=lam_init),
        grid=(nb, A_HEADS),
        in_specs=[pl.BlockSpec((nq, hw), lambda b, h: (b, h)),
                  pl.BlockSpec((1, past, hw), lambda b, h: (b, 0, h)),
                  pl.BlockSpec((1, past, hw), lambda b, h: (b, 0, h)),
                  pl.BlockSpec((nq, hw), lambda b, h: (b, h)),
                  pl.BlockSpec((nq, hw), lambda b, h: (b, h)),
                  pl.BlockSpec((4, HEAD_DIM), lambda b, h: (0, 0)),
                  pl.BlockSpec((1, hw), lambda b, h: (0, 0))],
        out_specs=pl.BlockSpec((nq, hw), lambda b, h: (b, h)),
        out_shape=jax.ShapeDtypeStruct((nb * nq, A_WIDTH), BF16),
        compiler_params=_params("parallel", "arbitrary"),
        name="diff_sample",
    )(q, kcache, vcache, knew, vnew, lam_p, g)


def _stick_sample_kernel(q_ref, kc_ref, vc_ref, kn_ref, vn_ref, g_ref, o_ref, *, past, nq, blk):
    q = q_ref[...]
    kc = kc_ref[0].astype(BF16)
    vc = vc_ref[0].astype(BF16)
    zn = _dot_nt(q, kn_ref[...])
    lbn = _log_sigmoid(zn)
    mask = (lax.broadcasted_iota(jnp.int32, (nq, nq), 1) < lax.broadcasted_iota(jnp.int32, (nq, nq), 0))
    l1n = jnp.where(mask, lbn - zn, 0.0)
    tail_n = jnp.dot(l1n.astype(BF16), _strict_upper(nq), preferred_element_type=F32)
    an = jnp.where(mask, jnp.exp(lbn + tail_n), 0.0)
    acc = jnp.dot(an.astype(BF16), vn_ref[...], preferred_element_type=F32)
    carry = jnp.sum(l1n, axis=-1, keepdims=True)
    zc = _dot_nt(q, kc)
    lbc = _log_sigmoid(zc)
    l1c = lbc - zc
    upper = _strict_upper(blk)
    for j in reversed(range(past // blk)):
        sl = slice(j * blk, (j + 1) * blk)
        l1 = l1c[:, sl]
        tail = jnp.dot(l1.astype(BF16), upper, preferred_element_type=F32) + carry
        a = jnp.exp(lbc[:, sl] + tail)
        acc = acc + jnp.dot(a.astype(BF16), vc[sl, :], preferred_element_type=F32)
        carry = carry + jnp.sum(l1, axis=-1, keepdims=True)
    o_ref[...] = _head_rmsnorm(acc, g_ref[...]).astype(o_ref.dtype)


def _stick_sample(q, kcache, vcache, knew, vnew, g, nq, blk=256):
    nb, past, _ = kcache.shape
    return pl.pallas_call(
        functools.partial(_stick_sample_kernel, past=past, nq=nq, blk=blk),
        grid=(nb, B_HEADS),
        in_specs=[pl.BlockSpec((nq, HEAD_DIM), lambda b, h: (b, h)),
                  pl.BlockSpec((1, past, HEAD_DIM), lambda b, h: (b, 0, h)),
                  pl.BlockSpec((1, past, HEAD_DIM), lambda b, h: (b, 0, h)),
                  pl.BlockSpec((nq, HEAD_DIM), lambda b, h: (b, h)),
                  pl.BlockSpec((nq, HEAD_DIM), lambda b, h: (b, h)),
                  pl.BlockSpec((1, HEAD_DIM), lambda b, h: (0, 0))],
        out_specs=pl.BlockSpec((nq, HEAD_DIM), lambda b, h: (b, h)),
        out_shape=jax.ShapeDtypeStruct((nb * nq, B_WIDTH), BF16),
        compiler_params=_params("parallel", "arbitrary"),
        name="stick_sample",
    )(q, kcache, vcache, knew, vnew, g)


def _project_all(x, g_attn, w_in, base, period, tm):
    m = x.shape[0]
    h = _rmsnorm(x, g_attn, BF16, tm=min(tm, 256))
    tabs = _rope_tables(m, base, period, tm=min(tm, 256))
    n = A_WIDTH
    kw = dict(tm=tm)
    (aq,) = _inproj(h, w_in, tabs, 0, n, rope=True, scale=SCALE, want_f32=False, want_bf16=True, **kw)
    ak32, ak = _inproj(h, w_in, tabs, 1, n, rope=True, scale=1.0, want_f32=True, want_bf16=True, **kw)
    av32, av = _inproj(h, w_in, tabs, 2, n, rope=False, scale=1.0, want_f32=True, want_bf16=True, **kw)
    (bq,) = _inproj(h, w_in, tabs, 3, n, rope=False, scale=SCALE, want_f32=False, want_bf16=True, **kw)
    bk32, bk = _inproj(h, w_in, tabs, 4, n, rope=False, scale=1.0, want_f32=True, want_bf16=True, **kw)
    bv32, bv = _inproj(h, w_in, tabs, 5, n, rope=False, scale=1.0, want_f32=True, want_bf16=True, **kw)
    return (aq, ak, av, bq, bk, bv), (ak32, av32, bk32, bv32)


def _finish(x, oa, ob, w_out, g_mlp, w_up, w_down, g_final, tm):
    o = jnp.concatenate([oa, ob], axis=1)
    x1 = _mm_residual(o, w_out, x, tm=tm)
    h2 = _rmsnorm(x1, g_mlp, BF16, tm=min(tm, 256))
    u = _mm_relu2(h2, w_up, tm=tm)
    x2 = _mm_residual(u, w_down, x1, tm=tm)
    return _rmsnorm(x2, g_final, F32, tm=min(tm, 256))


def kernel(x_prompt, x_sample, cache_a_k, cache_a_v, cache_b_k, cache_b_v, g_attn, w_in, lambda_q1, lambda_k1,
           lambda_q2, lambda_k2, g_subln, g_sb, w_out, g_mlp, w_up, w_down, g_final):
    depth = w_in.shape[0]
    assert depth == 1
    nb_p, seq, d = x_prompt.shape
    nb_s, dec, _ = x_sample.shape
    past = cache_a_k.shape[2]
    assert nb_p == 1 and d == D_MODEL

    l = 0
    lam_init = 0.8 - 0.6 * math.exp(-0.3 * l)
    w_in_b = w_in[l].astype(BF16)
    w_out_b = w_out[l].astype(BF16)
    w_up_b = w_up[l].astype(BF16)
    w_down_b = w_down[l].astype(BF16)
    lam_p = jnp.stack([lambda_q1[l], lambda_k1[l], lambda_q2[l], lambda_k2[l]]).astype(F32)
    g_sub = g_subln[l].reshape(1, 2 * HEAD_DIM)
    g_b = g_sb[l].reshape(1, HEAD_DIM)

    xp = x_prompt.reshape(seq, d)
    (aq, ak, av, bq, bk, bv), p32 = _project_all(xp, g_attn[l], w_in_b, 0, seq, tm=1024)
    oa = _diff_prompt(aq, ak, av, lam_p, g_sub, lam_init)
    ob = _stick_prompt(bq, bk, bv, g_b)
    y_prompt = _finish(xp, oa, ob, w_out_b, g_mlp[l], w_up_b, w_down_b, g_final, tm=1024)

    xs = x_sample.reshape(nb_s * dec, d)
    (aq, ak, av, bq, bk, bv), s32 = _project_all(xs, g_attn[l], w_in_b, past, dec, tm=256)
    oa = _diff_sample(aq, cache_a_k[l].reshape(nb_s, past, A_WIDTH), cache_a_v[l].reshape(nb_s, past, A_WIDTH),
                      ak, av, lam_p, g_sub, lam_init, dec)
    ob = _stick_sample(bq, cache_b_k[l].reshape(nb_s, past, B_WIDTH), cache_b_v[l].reshape(nb_s, past, B_WIDTH),
                       bk, bv, g_b, dec)
    y_sample = _finish(xs, oa, ob, w_out_b, g_mlp[l], w_up_b, w_down_b, g_final, tm=256)

    return (y_prompt.reshape(nb_p, seq, d), y_sample.reshape(nb_s, dec, d),
            p32[0].reshape(1, nb_p, seq, A_HEADS, 2, HEAD_DIM), p32[1].reshape(1, nb_p, seq, A_HEADS, 2 * HEAD_DIM),
            p32[2].reshape(1, nb_p, seq, B_HEADS, HEAD_DIM), p32[3].reshape(1, nb_p, seq, B_HEADS, HEAD_DIM),
            s32[0].reshape(1, nb_s, dec, A_HEADS, 2, HEAD_DIM), s32[1].reshape(1, nb_s, dec, A_HEADS, 2 * HEAD_DIM),
            s32[2].reshape(1, nb_s, dec, B_HEADS, HEAD_DIM), s32[3].reshape(1, nb_s, dec, B_HEADS, HEAD_DIM))
```

```python
import functools
import math

import jax
import jax.numpy as jnp
from jax import lax
from jax.experimental import pallas as pl
from jax.experimental.pallas import tpu as pltpu

D_MODEL = 4096
HEAD_DIM = 128
A_HEADS = 8
B_HEADS = 16
A_WIDTH = A_HEADS * 2 * HEAD_DIM
B_WIDTH = B_HEADS * HEAD_DIM
ROT_DIM = HEAD_DIM // 4
ROPE_THETA = 500000.0
CHUNK = 64
D_FF = 4 * D_MODEL
EPS = 1e-6
NEG = -1e30
SCALE = HEAD_DIM ** -0.5
LOG2E = math.log2(math.e)
LANES = 128
SUBLANES = 8
UNDERFLOW_LOG = -110.0

VMEM_LIMIT_BYTES = 56 * 1024 * 1024

F32 = jnp.float32
BF16 = jnp.bfloat16


def _params(*sem):
    return pltpu.CompilerParams(dimension_semantics=sem, vmem_limit_bytes=VMEM_LIMIT_BYTES)


def _rmsnorm_kernel(x_ref, g_ref, o_ref):
    x = x_ref[...]
    ms = jnp.mean(x * x, axis=-1, keepdims=True)
    o_ref[...] = (x * lax.rsqrt(ms + EPS) * g_ref[...]).astype(o_ref.dtype)


def _rmsnorm(x, g, out_dtype, tm=256):
    m, d = x.shape
    return pl.pallas_call(
        _rmsnorm_kernel,
        grid=(m // tm,),
        in_specs=[pl.BlockSpec((tm, d), lambda i: (i, 0)), pl.BlockSpec((1, d), lambda i: (0, 0))],
        out_specs=pl.BlockSpec((tm, d), lambda i: (i, 0)),
        out_shape=jax.ShapeDtypeStruct((m, d), out_dtype),
        compiler_params=_params("arbitrary"),
        name="rmsnorm",
    )(x, g.reshape(1, d))


def _rope_table_kernel(inv_ref, c_ref, s1_ref, s2_ref, *, tm, base, period):
    half = ROT_DIM // 2
    row = lax.broadcasted_iota(jnp.int32, (tm, HEAD_DIM), 0) + pl.program_id(0) * tm
    lane = lax.broadcasted_iota(jnp.int32, (tm, HEAD_DIM), 1)
    pos = (base + row % period).astype(F32)
    ang = pos * inv_ref[...]
    cos = jnp.cos(ang)
    sin = jnp.sin(ang)
    c_ref[...] = jnp.where(lane < ROT_DIM, cos, 1.0)
    s1_ref[...] = jnp.where(lane < half, -sin, 0.0)
    s2_ref[...] = jnp.where((lane >= half) & (lane < ROT_DIM), sin, 0.0)


def _rope_tables(m, base, period, tm=256):
    half = ROT_DIM // 2
    inv = 1.0 / (ROPE_THETA ** (jnp.arange(0, ROT_DIM, 2, dtype=F32) / ROT_DIM))
    inv_lane = jnp.concatenate([inv, inv, jnp.zeros((HEAD_DIM - ROT_DIM,), F32)]).reshape(1, HEAD_DIM)
    assert inv.shape[0] == half
    spec = pl.BlockSpec((tm, HEAD_DIM), lambda i: (i, 0))
    shp = jax.ShapeDtypeStruct((m, HEAD_DIM), F32)
    return pl.pallas_call(
        functools.partial(_rope_table_kernel, tm=tm, base=base, period=period),
        grid=(m // tm,),
        in_specs=[pl.BlockSpec((1, HEAD_DIM), lambda i: (0, 0))],
        out_specs=[spec, spec, spec],
        out_shape=[shp, shp, shp],
        compiler_params=_params("arbitrary"),
        name="rope_tables",
    )(inv_lane)


def _mm_accumulate(x_ref, w_ref, acc_ref):
    @pl.when(pl.program_id(2) == 0)
    def _():
        acc_ref[...] = jnp.zeros_like(acc_ref)

    acc_ref[...] += jnp.dot(x_ref[...], w_ref[...], preferred_element_type=F32)


def _inproj_kernel(x_ref, w_ref, c_ref, s1_ref, s2_ref, *rest, nk, rope, scale, want_f32, want_bf16):
    outs, acc_ref = rest[:-1], rest[-1]
    _mm_accumulate(x_ref, w_ref, acc_ref)

    @pl.when(pl.program_id(2) == nk - 1)
    def _():
        tn = acc_ref.shape[1]
        for hh in range(tn // HEAD_DIM):
            sl = slice(hh * HEAD_DIM, (hh + 1) * HEAD_DIM)
            y = acc_ref[:, sl]
            if rope:
                half = ROT_DIM // 2
                y = (y * c_ref[...]
                     + pltpu.roll(y, HEAD_DIM - half, 1) * s1_ref[...]
                     + pltpu.roll(y, half, 1) * s2_ref[...])
            k = 0
            if want_f32:
                outs[k][:, sl] = y
                k += 1
            if want_bf16:
                outs[k][:, sl] = (y * scale).astype(BF16)


def _inproj(h, w, tabs, col_block, n, *, rope, scale, want_f32, want_bf16, tm, tn=1024, tk=1024):
    m, kdim = h.shape
    nk = kdim // tk
    off = col_block * (n // tn)
    tab_spec = pl.BlockSpec((tm, HEAD_DIM), lambda i, j, k: (i, 0))
    out_specs, out_shape = [], []
    if want_f32:
        out_specs.append(pl.BlockSpec((tm, tn), lambda i, j, k: (i, j)))
        out_shape.append(jax.ShapeDtypeStruct((m, n), F32))
    if want_bf16:
        out_specs.append(pl.BlockSpec((tm, tn), lambda i, j, k: (i, j)))
        out_shape.append(jax.ShapeDtypeStruct((m, n), BF16))
    return pl.pallas_call(
        functools.partial(_inproj_kernel, nk=nk, rope=rope, scale=scale, want_f32=want_f32, want_bf16=want_bf16),
        grid=(m // tm, n // tn, nk),
        in_specs=[pl.BlockSpec((tm, tk), lambda i, j, k: (i, k)),
                  pl.BlockSpec((tk, tn), lambda i, j, k: (k, j + off)),
                  tab_spec, tab_spec, tab_spec],
        out_specs=out_specs,
        out_shape=out_shape,
        scratch_shapes=[pltpu.VMEM((tm, tn), F32)],
        compiler_params=_params("parallel", "parallel", "arbitrary"),
        name="inproj",
    )(h, w, *tabs)


def _mm_residual_kernel(x_ref, w_ref, r_ref, o_ref, acc_ref, *, nk):
    _mm_accumulate(x_ref, w_ref, acc_ref)

    @pl.when(pl.program_id(2) == nk - 1)
    def _():
        o_ref[...] = r_ref[...] + acc_ref[...]


def _mm_residual(x, w, r, *, tm, tn=1024, tk=1024):
    m, kdim = x.shape
    n = w.shape[1]
    nk = kdim // tk
    return pl.pallas_call(
        functools.partial(_mm_residual_kernel, nk=nk),
        grid=(m // tm, n // tn, nk),
        in_specs=[pl.BlockSpec((tm, tk), lambda i, j, k: (i, k)),
                  pl.BlockSpec((tk, tn), lambda i, j, k: (k, j)),
                  pl.BlockSpec((tm, tn), lambda i, j, k: (i, j))],
        out_specs=pl.BlockSpec((tm, tn), lambda i, j, k: (i, j)),
        out_shape=jax.ShapeDtypeStruct((m, n), F32),
        scratch_shapes=[pltpu.VMEM((tm, tn), F32)],
        compiler_params=_params("parallel", "parallel", "arbitrary"),
        name="mm_residual",
    )(x, w, r)


def _mm_relu2_kernel(x_ref, w_ref, o_ref, acc_ref, *, nk):
    _mm_accumulate(x_ref, w_ref, acc_ref)

    @pl.when(pl.program_id(2) == nk - 1)
    def _():
        u = jnp.maximum(acc_ref[...], 0.0)
        o_ref[...] = (u * u).astype(o_ref.dtype)


def _mm_relu2(x, w, *, tm, tn=1024, tk=1024):
    m, kdim = x.shape
    n = w.shape[1]
    nk = kdim // tk
    return pl.pallas_call(
        functools.partial(_mm_relu2_kernel, nk=nk),
        grid=(m // tm, n // tn, nk),
        in_specs=[pl.BlockSpec((tm, tk), lambda i, j, k: (i, k)),
                  pl.BlockSpec((tk, tn), lambda i, j, k: (k, j))],
        out_specs=pl.BlockSpec((tm, tn), lambda i, j, k: (i, j)),
        out_shape=jax.ShapeDtypeStruct((m, n), BF16),
        scratch_shapes=[pltpu.VMEM((tm, tn), F32)],
        compiler_params=_params("parallel", "parallel", "arbitrary"),
        name="mm_relu2",
    )(x, w)


def _lambda(lam_ref, lam_init):
    lp = lam_ref[...]
    s1 = jnp.sum(lp[0:1] * lp[1:2], axis=-1, keepdims=True)
    s2 = jnp.sum(lp[2:3] * lp[3:4], axis=-1, keepdims=True)
    return jnp.exp(s1) - jnp.exp(s2) + lam_init


def _head_rmsnorm(o, g, mult=None):
    y = o * lax.rsqrt(jnp.mean(o * o, axis=-1, keepdims=True) + EPS) * g
    if mult is not None:
        y = y * mult
    return y


def _dot_nt(a, b):
    return lax.dot_general(a, b, (((1,), (1,)), ((), ())), preferred_element_type=F32)


def _log_sigmoid(z):
    return jnp.minimum(z, 0.0) - jnp.log(1.0 + jnp.exp(-jnp.abs(z)))


def _strict_upper(n):
    r = lax.broadcasted_iota(jnp.int32, (n, n), 0)
    c = lax.broadcasted_iota(jnp.int32, (n, n), 1)
    return (r > c).astype(BF16)


def _lane_tile(x, width):
    return jnp.tile(x, (1, width // x.shape[1]))


def _lane_fold(x):
    out = x[:, :LANES]
    for i in range(1, x.shape[1] // LANES):
        out = out + x[:, i * LANES:(i + 1) * LANES]
    return out


def _diff_prompt_kernel(q_ref, k_ref, v_ref, lam_ref, g_ref, o_ref, m_sc, l_sc, acc_sc, *, tq, tk, lam_init):
    qi = pl.program_id(1)
    hw = 2 * HEAD_DIM

    def block(kstart, width, first):
        kb = k_ref[pl.ds(kstart, width), :]
        vb = v_ref[pl.ds(kstart, width), :]
        for c in range(2):
            sl = slice(c * HEAD_DIM, (c + 1) * HEAD_DIM)
            s = _dot_nt(q_ref[:, sl], kb[:, sl])
            if first:
                qc = lax.broadcasted_iota(jnp.int32, (tq, width), 0) // CHUNK
                kc = lax.broadcasted_iota(jnp.int32, (tq, width), 1) // CHUNK
                s = jnp.where(kc <= qc, s, NEG)
                m_new = jnp.broadcast_to(jnp.max(s, axis=1, keepdims=True), (tq, LANES))
                p = jnp.exp2(s - _lane_tile(m_new, width))
                l_sc[c] = _lane_fold(p)
                acc_sc[c] = jnp.dot(p.astype(BF16), vb, preferred_element_type=F32)
            else:
                m_old = m_sc[c]
                m_new = jnp.maximum(m_old, jnp.max(s, axis=1, keepdims=True))
                alpha = jnp.exp2(m_old - m_new)
                p = jnp.exp2(s - _lane_tile(m_new, width))
                l_sc[c] = alpha * l_sc[c] + _lane_fold(p)
                acc_sc[c] = _lane_tile(alpha, hw) * acc_sc[c] + jnp.dot(p.astype(BF16), vb,
                                                                         preferred_element_type=F32)
            m_sc[c] = m_new

    block(pl.multiple_of(qi * tq, tq), tq, True)

    def body(j, carry):
        block(pl.multiple_of(j * tk, tk), tk, False)
        return carry

    lax.fori_loop(0, qi * (tq // tk), body, 0)

    lam = _lambda(lam_ref, lam_init)
    l0 = jnp.sum(l_sc[0], axis=1, keepdims=True)
    l1 = jnp.sum(l_sc[1], axis=1, keepdims=True)
    o = acc_sc[0] / l0 - lam * (acc_sc[1] / l1)
    o_ref[...] = _head_rmsnorm(o, g_ref[...], 1.0 - lam_init).astype(o_ref.dtype)


def _diff_prompt(q, k, v, lam_p, g, lam_init, tq=512, tk=512):
    t = q.shape[0]
    hw = 2 * HEAD_DIM
    return pl.pallas_call(
        functools.partial(_diff_prompt_kernel, tq=tq, tk=tk, lam_init=lam_init),
        grid=(A_HEADS, t // tq),
        in_specs=[pl.BlockSpec((tq, hw), lambda h, i: (i, h)),
                  pl.BlockSpec((t, hw), lambda h, i: (0, h)),
                  pl.BlockSpec((t, hw), lambda h, i: (0, h)),
                  pl.BlockSpec((4, HEAD_DIM), lambda h, i: (0, 0)),
                  pl.BlockSpec((1, hw), lambda h, i: (0, 0))],
        out_specs=pl.BlockSpec((tq, hw), lambda h, i: (i, h)),
        out_shape=jax.ShapeDtypeStruct((t, A_WIDTH), BF16),
        scratch_shapes=[pltpu.VMEM((2, tq, LANES), F32), pltpu.VMEM((2, tq, LANES), F32),
                        pltpu.VMEM((2, tq, hw), F32)],
        compiler_params=_params("parallel", "arbitrary"),
        name="diff_prompt",
    )(q, k, v, lam_p, g)


def _cumsum_matrix(n):
    r = lax.broadcasted_iota(jnp.int32, (n, n + LANES), 0)
    c = lax.broadcasted_iota(jnp.int32, (n, n + LANES), 1)
    return ((r > c) | (c >= n)).astype(BF16)


def _stick_prompt_kernel(q_ref, k_ref, v_ref, g_ref, o_ref, carry_sc, acc_sc, *, tq, nh):
    qi = pl.program_id(1)
    cum = _cumsum_matrix(tq)

    def block(kstart, first):
        for hh in range(nh):
            sl = slice(hh * HEAD_DIM, (hh + 1) * HEAD_DIM)
            kb = k_ref[pl.ds(kstart, tq), sl]
            vb = v_ref[pl.ds(kstart, tq), sl]
            z = _dot_nt(q_ref[:, sl], kb)
            lb = _log_sigmoid(z)
            l1 = lb - z
            if first:
                mask = (lax.broadcasted_iota(jnp.int32, (tq, tq), 1)
                        < lax.broadcasted_iota(jnp.int32, (tq, tq), 0))
                l1 = jnp.where(mask, l1, 0.0)
            sums = jnp.dot(l1.astype(BF16), cum, preferred_element_type=F32)
            if first:
                a = jnp.where(mask, jnp.exp(lb + sums[:, :tq]), 0.0)
                carry_sc[hh] = sums[:, tq:]
                acc_sc[hh] = jnp.dot(a.astype(BF16), vb, preferred_element_type=F32)
            else:
                a = jnp.exp(lb + sums[:, :tq] + _lane_tile(carry_sc[hh], tq))
                carry_sc[hh] += sums[:, tq:]
                acc_sc[hh] += jnp.dot(a.astype(BF16), vb, preferred_element_type=F32)

    block(pl.multiple_of(qi * tq, tq), True)

    def cond(st):
        j, cmax = st
        return (j >= 0) & (cmax > UNDERFLOW_LOG)

    def body(st):
        j, _ = st
        block(pl.multiple_of(j * tq, tq), False)
        return j - 1, jnp.max(carry_sc[...])

    lax.while_loop(cond, body, (qi - 1, jnp.max(carry_sc[...])))
    for hh in range(nh):
        sl = slice(hh * HEAD_DIM, (hh + 1) * HEAD_DIM)
        o_ref[:, sl] = _head_rmsnorm(acc_sc[hh], g_ref[...]).astype(o_ref.dtype)


def _stick_prompt(q, k, v, g, tq=256, nh=2):
    t = q.shape[0]
    w = nh * HEAD_DIM
    return pl.pallas_call(
        functools.partial(_stick_prompt_kernel, tq=tq, nh=nh),
        grid=(B_HEADS // nh, t // tq),
        in_specs=[pl.BlockSpec((tq, w), lambda h, i: (i, h)),
                  pl.BlockSpec((t, w), lambda h, i: (0, h)),
                  pl.BlockSpec((t, w), lambda h, i: (0, h)),
                  pl.BlockSpec((1, HEAD_DIM), lambda h, i: (0, 0))],
        out_specs=pl.BlockSpec((tq, w), lambda h, i: (i, h)),
        out_shape=jax.ShapeDtypeStruct((t, B_WIDTH), BF16),
        scratch_shapes=[pltpu.VMEM((nh, tq, LANES), F32), pltpu.VMEM((nh, tq, HEAD_DIM), F32)],
        compiler_params=_params("parallel", "arbitrary"),
        name="stick_prompt",
    )(q, k, v, g)


def _diff_sample_kernel(q_ref, kc_ref, vc_ref, kn_ref, vn_ref, lam_ref, g_ref, o_ref, m_sc, l_sc, acc_sc,
                        *, past, nq, tk, lam_init):
    kb = pl.program_id(1)
    hw = 2 * HEAD_DIM
    qchunk = (past + lax.broadcasted_iota(jnp.int32, (nq, 1), 0)) // CHUNK

    def sweep(hc, k, v, kpos, first):
        s = _dot_nt(q_ref[:, hc * HEAD_DIM:(hc + 1) * HEAD_DIM], k)
        s = jnp.where(kpos // CHUNK <= qchunk, s, NEG)
        m_cur = jnp.broadcast_to(jnp.max(s, axis=1, keepdims=True), (nq, LANES))
        if first:
            m_new = m_cur
            p = jnp.exp2(s - m_new[:, :1])
            l_sc[hc] = jnp.broadcast_to(jnp.sum(p, axis=1, keepdims=True), (nq, LANES))
            acc_sc[hc] = jnp.dot(p.astype(BF16), v, preferred_element_type=F32)
        else:
            m_old = m_sc[hc]
            m_new = jnp.maximum(m_old, m_cur)
            alpha = jnp.exp2(m_old - m_new)
            p = jnp.exp2(s - m_new[:, :1])
            l_sc[hc] = alpha * l_sc[hc] + jnp.sum(p, axis=1, keepdims=True)
            acc_sc[hc] = _lane_tile(alpha, hw) * acc_sc[hc] + jnp.dot(p.astype(BF16), v, preferred_element_type=F32)
        m_sc[hc] = m_new

    @pl.when(kb == 0)
    def _():
        kpos = past + lax.broadcasted_iota(jnp.int32, (nq, nq), 1)
        for hc in range(2 * A_HEADS):
            h = hc // 2
            sweep(hc, kn_ref[:, hc * HEAD_DIM:(hc + 1) * HEAD_DIM], vn_ref[:, h * hw:(h + 1) * hw], kpos, True)

    kpos = kb * tk + lax.broadcasted_iota(jnp.int32, (nq, tk), 1)
    for h in range(A_HEADS):
        v = vc_ref[:, h].reshape(tk, hw).astype(BF16)
        for c in range(2):
            sweep(2 * h + c, kc_ref[:, 2 * h + c].reshape(tk, HEAD_DIM).astype(BF16), v, kpos, False)

    @pl.when(kb == pl.num_programs(1) - 1)
    def _():
        lam = _lambda(lam_ref, lam_init)
        for h in range(A_HEADS):
            o = (acc_sc[2 * h] / _lane_tile(l_sc[2 * h], hw)
                 - lam * (acc_sc[2 * h + 1] / _lane_tile(l_sc[2 * h + 1], hw)))
            o_ref[:, h * hw:(h + 1) * hw] = _head_rmsnorm(o, g_ref[...], 1.0 - lam_init).astype(o_ref.dtype)


def _diff_sample(q, kcache, vcache, knew, vnew, lam_p, g, lam_init, nq, tk=512):
    nb = kcache.shape[0]
    past = kcache.shape[1] * SUBLANES
    hw = 2 * HEAD_DIM
    row = pl.BlockSpec((nq, A_WIDTH), lambda b, j: (b, 0))
    return pl.pallas_call(
        functools.partial(_diff_sample_kernel, past=past, nq=nq, tk=tk, lam_init=lam_init),
        grid=(nb, past // tk),
        in_specs=[row,
                  pl.BlockSpec((None, tk // SUBLANES, 2 * A_HEADS, SUBLANES, HEAD_DIM), lambda b, j: (b, j, 0, 0, 0)),
                  pl.BlockSpec((None, tk // SUBLANES, A_HEADS, SUBLANES, hw), lambda b, j: (b, j, 0, 0, 0)),
                  row, row,
                  pl.BlockSpec((4, HEAD_DIM), lambda b, j: (0, 0)),
                  pl.BlockSpec((1, hw), lambda b, j: (0, 0))],
        out_specs=row,
        out_shape=jax.ShapeDtypeStruct((nb * nq, A_WIDTH), BF16),
        scratch_shapes=[pltpu.VMEM((2 * A_HEADS, nq, LANES), F32), pltpu.VMEM((2 * A_HEADS, nq, LANES), F32),
                        pltpu.VMEM((2 * A_HEADS, nq, hw), F32)],
        compiler_params=_params("parallel", "arbitrary"),
        name="diff_sample",
    )(q, kcache, vcache, knew, vnew, lam_p, g)


def _stick_sample_kernel(q_ref, kc_ref, vc_ref, kn_ref, vn_ref, g_ref, o_ref, carry_sc, acc_sc, *, nq, tk):
    kb = pl.program_id(1)

    @pl.when(kb == 0)
    def _():
        mask = lax.broadcasted_iota(jnp.int32, (nq, nq), 1) < lax.broadcasted_iota(jnp.int32, (nq, nq), 0)
        upper = _strict_upper(nq)
        for h in range(B_HEADS):
            sl = slice(h * HEAD_DIM, (h + 1) * HEAD_DIM)
            z = _dot_nt(q_ref[:, sl], kn_ref[:, sl])
            lb = _log_sigmoid(z)
            l1 = jnp.where(mask, lb - z, 0.0)
            tail = jnp.dot(l1.astype(BF16), upper, preferred_element_type=F32)
            a = jnp.where(mask, jnp.exp(lb + tail), 0.0)
            carry_sc[h] = jnp.broadcast_to(jnp.sum(l1, axis=1, keepdims=True), (nq, LANES))
            acc_sc[h] = jnp.dot(a.astype(BF16), vn_ref[:, sl], preferred_element_type=F32)

    @pl.when(jnp.max(carry_sc[...]) > UNDERFLOW_LOG)
    def _():
        cum = _cumsum_matrix(tk)
        for h in range(B_HEADS):
            k = kc_ref[:, h].reshape(tk, HEAD_DIM).astype(BF16)
            z = _dot_nt(q_ref[:, h * HEAD_DIM:(h + 1) * HEAD_DIM], k)
            lb = _log_sigmoid(z)
            sums = jnp.dot((lb - z).astype(BF16), cum, preferred_element_type=F32)
            a = jnp.exp(lb + sums[:, :tk] + _lane_tile(carry_sc[h], tk))
            carry_sc[h] += sums[:, tk:]
            v = vc_ref[:, h].reshape(tk, HEAD_DIM).astype(BF16)
            acc_sc[h] += jnp.dot(a.astype(BF16), v, preferred_element_type=F32)

    @pl.when(kb == pl.num_programs(1) - 1)
    def _():
        for h in range(B_HEADS):
            o_ref[:, h * HEAD_DIM:(h + 1) * HEAD_DIM] = _head_rmsnorm(acc_sc[h], g_ref[...]).astype(o_ref.dtype)


def _stick_sample(q, kcache, vcache, knew, vnew, g, nq, tk=256):
    nb = kcache.shape[0]
    nkb = kcache.shape[1] * SUBLANES // tk
    row = pl.BlockSpec((nq, B_WIDTH), lambda b, j: (b, 0))
    cache = pl.BlockSpec((None, tk // SUBLANES, B_HEADS, SUBLANES, HEAD_DIM), lambda b, j: (b, nkb - 1 - j, 0, 0, 0))
    return pl.pallas_call(
        functools.partial(_stick_sample_kernel, nq=nq, tk=tk),
        grid=(nb, nkb),
        in_specs=[row, cache, cache, row, row, pl.BlockSpec((1, HEAD_DIM), lambda b, j: (0, 0))],
        out_specs=row,
        out_shape=jax.ShapeDtypeStruct((nb * nq, B_WIDTH), BF16),
        scratch_shapes=[pltpu.VMEM((B_HEADS, nq, LANES), F32), pltpu.VMEM((B_HEADS, nq, HEAD_DIM), F32)],
        compiler_params=_params("parallel", "arbitrary"),
        name="stick_sample",
    )(q, kcache, vcache, knew, vnew, g)


def _by_sublane_tile(cache):
    nb, t, nh, w = cache.shape
    return cache.reshape(nb, t // SUBLANES, SUBLANES, nh, w).transpose(0, 1, 3, 2, 4)


def _project_all(x, g_attn, w_in, base, period, tm):
    m = x.shape[0]
    h = _rmsnorm(x, g_attn, BF16, tm=min(tm, 256))
    tabs = _rope_tables(m, base, period, tm=min(tm, 256))
    n = A_WIDTH
    kw = dict(tm=tm)
    (aq,) = _inproj(h, w_in, tabs, 0, n, rope=True, scale=SCALE * LOG2E, want_f32=False, want_bf16=True, **kw)
    ak32, ak = _inproj(h, w_in, tabs, 1, n, rope=True, scale=1.0, want_f32=True, want_bf16=True, **kw)
    av32, av = _inproj(h, w_in, tabs, 2, n, rope=False, scale=1.0, want_f32=True, want_bf16=True, **kw)
    (bq,) = _inproj(h, w_in, tabs, 3, n, rope=False, scale=SCALE, want_f32=False, want_bf16=True, **kw)
    bk32, bk = _inproj(h, w_in, tabs, 4, n, rope=False, scale=1.0, want_f32=True, want_bf16=True, **kw)
    bv32, bv = _inproj(h, w_in, tabs, 5, n, rope=False, scale=1.0, want_f32=True, want_bf16=True, **kw)
    return (aq, ak, av, bq, bk, bv), (ak32, av32, bk32, bv32)


def _finish(x, oa, ob, w_out, g_mlp, w_up, w_down, g_final, tm):
    o = jnp.concatenate([oa, ob], axis=1)
    x1 = _mm_residual(o, w_out, x, tm=tm)
    h2 = _rmsnorm(x1, g_mlp, BF16, tm=min(tm, 256))
    u = _mm_relu2(h2, w_up, tm=tm)
    x2 = _mm_residual(u, w_down, x1, tm=tm)
    return _rmsnorm(x2, g_final, F32, tm=min(tm, 256))


def kernel(x_prompt, x_sample, cache_a_k, cache_a_v, cache_b_k, cache_b_v, g_attn, w_in, lambda_q1, lambda_k1,
           lambda_q2, lambda_k2, g_subln, g_sb, w_out, g_mlp, w_up, w_down, g_final):
    depth = w_in.shape[0]
    assert depth == 1
    nb_p, seq, d = x_prompt.shape
    nb_s, dec, _ = x_sample.shape
    past = cache_a_k.shape[2]
    assert nb_p == 1 and d == D_MODEL

    l = 0
    lam_init = 0.8 - 0.6 * math.exp(-0.3 * l)
    w_in_b = w_in[l].astype(BF16)
    w_out_b = w_out[l].astype(BF16)
    w_up_b = w_up[l].astype(BF16)
    w_down_b = w_down[l].astype(BF16)
    lam_p = jnp.stack([lambda_q1[l], lambda_k1[l], lambda_q2[l], lambda_k2[l]]).astype(F32)
    g_sub = g_subln[l].reshape(1, 2 * HEAD_DIM)
    g_b = g_sb[l].reshape(1, HEAD_DIM)

    xp = x_prompt.reshape(seq, d)
    (aq, ak, av, bq, bk, bv), p32 = _project_all(xp, g_attn[l], w_in_b, 0, seq, tm=1024)
    oa = _diff_prompt(aq, ak, av, lam_p, g_sub, lam_init)
    ob = _stick_prompt(bq, bk, bv, g_b)
    y_prompt = _finish(xp, oa, ob, w_out_b, g_mlp[l], w_up_b, w_down_b, g_final, tm=1024)

    xs = x_sample.reshape(nb_s * dec, d)
    (aq, ak, av, bq, bk, bv), s32 = _project_all(xs, g_attn[l], w_in_b, past, dec, tm=256)
    oa = _diff_sample(aq, _by_sublane_tile(cache_a_k[l].reshape(nb_s, past, 2 * A_HEADS, HEAD_DIM)),
                      _by_sublane_tile(cache_a_v[l]), ak, av, lam_p, g_sub, lam_init, dec)
    ob = _stick_sample(bq, _by_sublane_tile(cache_b_k[l]), _by_sublane_tile(cache_b_v[l]), bk, bv, g_b, dec)
    y_sample = _finish(xs, oa, ob, w_out_b, g_mlp[l], w_up_b, w_down_b, g_final, tm=256)

    return (y_prompt.reshape(nb_p, seq, d), y_sample.reshape(nb_s, dec, d),
            p32[0].reshape(1, nb_p, seq, A_HEADS, 2, HEAD_DIM), p32[1].reshape(1, nb_p, seq, A_HEADS, 2 * HEAD_DIM),
            p32[2].reshape(1, nb_p, seq, B_HEADS, HEAD_DIM), p32[3].reshape(1, nb_p, seq, B_HEADS, HEAD_DIM),
            s32[0].reshape(1, nb_s, dec, A_HEADS, 2, HEAD_DIM), s32[1].reshape(1, nb_s, dec, A_HEADS, 2 * HEAD_DIM),
            s32[2].reshape(1, nb_s, dec, B_HEADS, HEAD_DIM), s32[3].reshape(1, nb_s, dec, B_HEADS, HEAD_DIM))
```

```python
import functools
import math

import jax
import jax.numpy as jnp
from jax import lax
from jax.experimental import pallas as pl
from jax.experimental.pallas import tpu as pltpu

D_MODEL = 4096
HEAD_DIM = 128
A_HEADS = 8
B_HEADS = 16
A_WIDTH = A_HEADS * 2 * HEAD_DIM
B_WIDTH = B_HEADS * HEAD_DIM
ROT_DIM = HEAD_DIM // 4
ROPE_THETA = 500000.0
CHUNK = 64
D_FF = 4 * D_MODEL
EPS = 1e-6
NEG = -1e30
SCALE = HEAD_DIM ** -0.5
LOG2E = math.log2(math.e)
LANES = 128
SUBLANES = 8
UNDERFLOW_LOG = -110.0

VMEM_LIMIT_BYTES = 56 * 1024 * 1024

F32 = jnp.float32
BF16 = jnp.bfloat16


def _params(*sem):
    return pltpu.CompilerParams(dimension_semantics=sem, vmem_limit_bytes=VMEM_LIMIT_BYTES)


def _rmsnorm_kernel(x_ref, g_ref, o_ref):
    x = x_ref[...]
    ms = jnp.mean(x * x, axis=-1, keepdims=True)
    o_ref[...] = (x * lax.rsqrt(ms + EPS) * g_ref[...]).astype(o_ref.dtype)


def _rmsnorm(x, g, out_dtype, tm=256):
    m, d = x.shape
    return pl.pallas_call(
        _rmsnorm_kernel,
        grid=(m // tm,),
        in_specs=[pl.BlockSpec((tm, d), lambda i: (i, 0)), pl.BlockSpec((1, d), lambda i: (0, 0))],
        out_specs=pl.BlockSpec((tm, d), lambda i: (i, 0)),
        out_shape=jax.ShapeDtypeStruct((m, d), out_dtype),
        compiler_params=_params("arbitrary"),
        name="rmsnorm",
    )(x, g.reshape(1, d))


def _rope_table_kernel(inv_ref, c_ref, s1_ref, s2_ref, *, tm, base, period):
    half = ROT_DIM // 2
    row = lax.broadcasted_iota(jnp.int32, (tm, HEAD_DIM), 0) + pl.program_id(0) * tm
    lane = lax.broadcasted_iota(jnp.int32, (tm, HEAD_DIM), 1)
    pos = (base + row % period).astype(F32)
    ang = pos * inv_ref[...]
    cos = jnp.cos(ang)
    sin = jnp.sin(ang)
    c_ref[...] = jnp.where(lane < ROT_DIM, cos, 1.0)
    s1_ref[...] = jnp.where(lane < half, -sin, 0.0)
    s2_ref[...] = jnp.where((lane >= half) & (lane < ROT_DIM), sin, 0.0)


def _rope_tables(m, base, period, tm=256):
    half = ROT_DIM // 2
    inv = 1.0 / (ROPE_THETA ** (jnp.arange(0, ROT_DIM, 2, dtype=F32) / ROT_DIM))
    inv_lane = jnp.concatenate([inv, inv, jnp.zeros((HEAD_DIM - ROT_DIM,), F32)]).reshape(1, HEAD_DIM)
    assert inv.shape[0] == half
    spec = pl.BlockSpec((tm, HEAD_DIM), lambda i: (i, 0))
    shp = jax.ShapeDtypeStruct((m, HEAD_DIM), F32)
    return pl.pallas_call(
        functools.partial(_rope_table_kernel, tm=tm, base=base, period=period),
        grid=(m // tm,),
        in_specs=[pl.BlockSpec((1, HEAD_DIM), lambda i: (0, 0))],
        out_specs=[spec, spec, spec],
        out_shape=[shp, shp, shp],
        compiler_params=_params("arbitrary"),
        name="rope_tables",
    )(inv_lane)


def _epilogue_inproj(y, c0, extra, outs, *, rope, scale, want_f32, want_bf16):
    c_ref, s1_ref, s2_ref = extra
    half = ROT_DIM // 2
    for hh in range(y.shape[1] // HEAD_DIM):
        yh = y[:, hh * HEAD_DIM:(hh + 1) * HEAD_DIM]
        if rope:
            yh = (yh * c_ref[...]
                  + pltpu.roll(yh, HEAD_DIM - half, 1) * s1_ref[...]
                  + pltpu.roll(yh, half, 1) * s2_ref[...])
        sl = slice(c0 + hh * HEAD_DIM, c0 + (hh + 1) * HEAD_DIM)
        k = 0
        if want_f32:
            outs[k][:, sl] = yh
            k += 1
        if want_bf16:
            outs[k][:, sl] = (yh * scale).astype(BF16)


def _epilogue_residual(y, c0, extra, outs):
    (r_ref,), (o_ref,) = extra, outs
    sl = slice(c0, c0 + y.shape[1])
    o_ref[:, sl] = r_ref[:, sl] + y


def _epilogue_relu2(y, c0, extra, outs):
    (o_ref,) = outs
    u = jnp.maximum(y, 0.0)
    o_ref[:, c0:c0 + y.shape[1]] = (u * u).astype(o_ref.dtype)


def _mm_fullk_kernel(x_ref, w_ref, *rest, n_extra, epilogue, chunk):
    extra, outs = rest[:n_extra], rest[n_extra:]
    for c0 in range(0, w_ref.shape[1], chunk):
        y = jnp.dot(x_ref[...], w_ref[:, c0:c0 + chunk], preferred_element_type=F32)
        epilogue(y, c0, extra, outs)


def _mm_ksplit_kernel(x_ref, w_ref, *rest, n_extra, epilogue, chunk, emit_w):
    extra, outs, acc_ref = rest[:n_extra], rest[n_extra:-1], rest[-1]
    k = pl.program_id(2)

    @pl.when(k == 0)
    def _():
        acc_ref[...] = jnp.zeros_like(acc_ref)

    w = w_ref[...]
    if emit_w:
        outs, wb_ref = outs[:-1], outs[-1]
        w = w.astype(BF16)
        wb_ref[...] = w
    acc_ref[...] += jnp.dot(x_ref[...], w, preferred_element_type=F32)

    @pl.when(k == pl.num_programs(2) - 1)
    def _():
        for c0 in range(0, acc_ref.shape[1], chunk):
            epilogue(acc_ref[:, c0:c0 + chunk], c0, extra, outs)


def _matmul(x, w, epilogue, out_dtypes, *, tm, tn, tk=None, rows=(), tiles=(), n=None, col_block=0, emit_w=False,
            chunk=256, name):
    m, kdim = x.shape
    n = w.shape[1] if n is None else n
    off = col_block * (n // tn)
    out_shape = [jax.ShapeDtypeStruct((m, n), dt) for dt in out_dtypes]
    if tk is None:
        grid = (m // tm, n // tn)
        in_specs = ([pl.BlockSpec((tm, kdim), lambda i, j: (i, 0)), pl.BlockSpec((kdim, tn), lambda i, j: (0, j + off))]
                    + [pl.BlockSpec((tm, LANES), lambda i, j: (i, 0)) for _ in rows]
                    + [pl.BlockSpec((tm, tn), lambda i, j: (i, j)) for _ in tiles])
        out_specs = [pl.BlockSpec((tm, tn), lambda i, j: (i, j)) for _ in out_dtypes]
        body = functools.partial(_mm_fullk_kernel, n_extra=len(rows) + len(tiles), epilogue=epilogue, chunk=chunk)
        scratch, sem = [], ("parallel", "arbitrary")
        assert not emit_w
    else:
        grid = (m // tm, n // tn, kdim // tk)
        in_specs = ([pl.BlockSpec((tm, tk), lambda i, j, k: (i, k)), pl.BlockSpec((tk, tn), lambda i, j, k: (k, j + off))]
                    + [pl.BlockSpec((tm, LANES), lambda i, j, k: (i, 0)) for _ in rows]
                    + [pl.BlockSpec((tm, tn), lambda i, j, k: (i, j)) for _ in tiles])
        out_specs = [pl.BlockSpec((tm, tn), lambda i, j, k: (i, j)) for _ in out_dtypes]
        if emit_w:
            assert m == tm and w.dtype == F32
            out_specs.append(pl.BlockSpec((tk, tn), lambda i, j, k: (k, j)))
            out_shape.append(jax.ShapeDtypeStruct((kdim, n), BF16))
        body = functools.partial(_mm_ksplit_kernel, n_extra=len(rows) + len(tiles), epilogue=epilogue, chunk=chunk,
                                 emit_w=emit_w)
        scratch, sem = [pltpu.VMEM((tm, tn), F32)], ("parallel", "parallel", "arbitrary")
    return pl.pallas_call(body, grid=grid, in_specs=in_specs, out_specs=out_specs, out_shape=out_shape,
                          scratch_shapes=scratch, compiler_params=_params(*sem), name=name)(x, w, *rows, *tiles)


def _lambda(lam_ref, lam_init):
    lp = lam_ref[...]
    s1 = jnp.sum(lp[0:1] * lp[1:2], axis=-1, keepdims=True)
    s2 = jnp.sum(lp[2:3] * lp[3:4], axis=-1, keepdims=True)
    return jnp.exp(s1) - jnp.exp(s2) + lam_init


def _head_rmsnorm(o, g, mult=None):
    y = o * lax.rsqrt(jnp.mean(o * o, axis=-1, keepdims=True) + EPS) * g
    if mult is not None:
        y = y * mult
    return y


def _dot_nt(a, b):
    return lax.dot_general(a, b, (((1,), (1,)), ((), ())), preferred_element_type=F32)


def _log_sigmoid(z):
    return jnp.minimum(z, 0.0) - jnp.log(1.0 + jnp.exp(-jnp.abs(z)))


def _strict_upper(n):
    r = lax.broadcasted_iota(jnp.int32, (n, n), 0)
    c = lax.broadcasted_iota(jnp.int32, (n, n), 1)
    return (r > c).astype(BF16)


def _lane_tile(x, width):
    return jnp.tile(x, (1, width // x.shape[1]))


def _lane_fold(x):
    out = x[:, :LANES]
    for i in range(1, x.shape[1] // LANES):
        out = out + x[:, i * LANES:(i + 1) * LANES]
    return out


def _diff_prompt_kernel(q_ref, k_ref, v_ref, lam_ref, g_ref, o_ref, m_sc, l_sc, acc_sc, *, tq, tk, lam_init):
    qi = pl.program_id(1)
    hw = 2 * HEAD_DIM

    def block(kstart, width, first):
        kb = k_ref[pl.ds(kstart, width), :]
        vb = v_ref[pl.ds(kstart, width), :]
        for c in range(2):
            sl = slice(c * HEAD_DIM, (c + 1) * HEAD_DIM)
            s = _dot_nt(q_ref[:, sl], kb[:, sl])
            if first:
                qc = lax.broadcasted_iota(jnp.int32, (tq, width), 0) // CHUNK
                kc = lax.broadcasted_iota(jnp.int32, (tq, width), 1) // CHUNK
                s = jnp.where(kc <= qc, s, NEG)
                m_new = jnp.broadcast_to(jnp.max(s, axis=1, keepdims=True), (tq, LANES))
                p = jnp.exp2(s - _lane_tile(m_new, width))
                l_sc[c] = _lane_fold(p)
                acc_sc[c] = jnp.dot(p.astype(BF16), vb, preferred_element_type=F32)
            else:
                m_old = m_sc[c]
                m_new = jnp.maximum(m_old, jnp.max(s, axis=1, keepdims=True))
                alpha = jnp.exp2(m_old - m_new)
                p = jnp.exp2(s - _lane_tile(m_new, width))
                l_sc[c] = alpha * l_sc[c] + _lane_fold(p)
                acc_sc[c] = _lane_tile(alpha, hw) * acc_sc[c] + jnp.dot(p.astype(BF16), vb,
                                                                         preferred_element_type=F32)
            m_sc[c] = m_new

    block(pl.multiple_of(qi * tq, tq), tq, True)

    def body(j, carry):
        block(pl.multiple_of(j * tk, tk), tk, False)
        return carry

    nwide = (qi * tq) // tk
    lax.fori_loop(0, nwide, body, 0)
    if tk != tq:
        @pl.when(nwide * tk < qi * tq)
        def _():
            block(pl.multiple_of(nwide * tk, tq), tq, False)

    lam = _lambda(lam_ref, lam_init)
    l0 = jnp.sum(l_sc[0], axis=1, keepdims=True)
    l1 = jnp.sum(l_sc[1], axis=1, keepdims=True)
    o = acc_sc[0] / l0 - lam * (acc_sc[1] / l1)
    o_ref[...] = _head_rmsnorm(o, g_ref[...], 1.0 - lam_init).astype(o_ref.dtype)


def _diff_prompt(q, k, v, lam_p, g, lam_init, tq=512, tk=1024):
    t = q.shape[0]
    assert tk in (tq, 2 * tq)
    hw = 2 * HEAD_DIM
    return pl.pallas_call(
        functools.partial(_diff_prompt_kernel, tq=tq, tk=tk, lam_init=lam_init),
        grid=(A_HEADS, t // tq),
        in_specs=[pl.BlockSpec((tq, hw), lambda h, i: (i, h)),
                  pl.BlockSpec((t, hw), lambda h, i: (0, h)),
                  pl.BlockSpec((t, hw), lambda h, i: (0, h)),
                  pl.BlockSpec((4, HEAD_DIM), lambda h, i: (0, 0)),
                  pl.BlockSpec((1, hw), lambda h, i: (0, 0))],
        out_specs=pl.BlockSpec((tq, hw), lambda h, i: (i, h)),
        out_shape=jax.ShapeDtypeStruct((t, A_WIDTH), BF16),
        scratch_shapes=[pltpu.VMEM((2, tq, LANES), F32), pltpu.VMEM((2, tq, LANES), F32),
                        pltpu.VMEM((2, tq, hw), F32)],
        compiler_params=_params("parallel", "arbitrary"),
        name="diff_prompt",
    )(q, k, v, lam_p, g)


def _cumsum_matrix(n):
    r = lax.broadcasted_iota(jnp.int32, (n, n + LANES), 0)
    c = lax.broadcasted_iota(jnp.int32, (n, n + LANES), 1)
    return ((r > c) | (c >= n)).astype(BF16)


def _stick_prompt_kernel(q_ref, k_ref, v_ref, g_ref, o_ref, carry_sc, acc_sc, *, tq, nh):
    qi = pl.program_id(1)
    cum = _cumsum_matrix(tq)

    def block(kstart, first):
        for hh in range(nh):
            sl = slice(hh * HEAD_DIM, (hh + 1) * HEAD_DIM)
            kb = k_ref[pl.ds(kstart, tq), sl]
            vb = v_ref[pl.ds(kstart, tq), sl]
            z = _dot_nt(q_ref[:, sl], kb)
            lb = _log_sigmoid(z)
            l1 = lb - z
            if first:
                mask = (lax.broadcasted_iota(jnp.int32, (tq, tq), 1)
                        < lax.broadcasted_iota(jnp.int32, (tq, tq), 0))
                l1 = jnp.where(mask, l1, 0.0)
            sums = jnp.dot(l1.astype(BF16), cum, preferred_element_type=F32)
            if first:
                a = jnp.where(mask, jnp.exp(lb + sums[:, :tq]), 0.0)
                carry_sc[hh] = sums[:, tq:]
                acc_sc[hh] = jnp.dot(a.astype(BF16), vb, preferred_element_type=F32)
            else:
                a = jnp.exp(lb + sums[:, :tq] + _lane_tile(carry_sc[hh], tq))
                carry_sc[hh] += sums[:, tq:]
                acc_sc[hh] += jnp.dot(a.astype(BF16), vb, preferred_element_type=F32)

    block(pl.multiple_of(qi * tq, tq), True)

    def cond(st):
        j, cmax = st
        return (j >= 0) & (cmax > UNDERFLOW_LOG)

    def body(st):
        j, _ = st
        block(pl.multiple_of(j * tq, tq), False)
        return j - 1, jnp.max(carry_sc[...])

    lax.while_loop(cond, body, (qi - 1, jnp.max(carry_sc[...])))
    for hh in range(nh):
        sl = slice(hh * HEAD_DIM, (hh + 1) * HEAD_DIM)
        o_ref[:, sl] = _head_rmsnorm(acc_sc[hh], g_ref[...]).astype(o_ref.dtype)


def _stick_prompt(q, k, v, g, tq=256, nh=4):
    t = q.shape[0]
    w = nh * HEAD_DIM
    return pl.pallas_call(
        functools.partial(_stick_prompt_kernel, tq=tq, nh=nh),
        grid=(B_HEADS // nh, t // tq),
        in_specs=[pl.BlockSpec((tq, w), lambda h, i: (i, h)),
                  pl.BlockSpec((t, w), lambda h, i: (0, h)),
                  pl.BlockSpec((t, w), lambda h, i: (0, h)),
                  pl.BlockSpec((1, HEAD_DIM), lambda h, i: (0, 0))],
        out_specs=pl.BlockSpec((tq, w), lambda h, i: (i, h)),
        out_shape=jax.ShapeDtypeStruct((t, B_WIDTH), BF16),
        scratch_shapes=[pltpu.VMEM((nh, tq, LANES), F32), pltpu.VMEM((nh, tq, HEAD_DIM), F32)],
        compiler_params=_params("parallel", "arbitrary"),
        name="stick_prompt",
    )(q, k, v, g)


def _diff_sample_kernel(q_ref, kc_ref, vc_ref, kn_ref, vn_ref, lam_ref, g_ref, o_ref, m_sc, l_sc, acc_sc,
                        *, past, nq, tk, lam_init):
    kb = pl.program_id(1)
    hw = 2 * HEAD_DIM
    qchunk = (past + lax.broadcasted_iota(jnp.int32, (nq, 1), 0)) // CHUNK

    def sweep(hc, k, v, kpos, first):
        s = _dot_nt(q_ref[:, hc * HEAD_DIM:(hc + 1) * HEAD_DIM], k)
        s = jnp.where(kpos // CHUNK <= qchunk, s, NEG)
        m_cur = jnp.broadcast_to(jnp.max(s, axis=1, keepdims=True), (nq, LANES))
        if first:
            m_new = m_cur
            p = jnp.exp2(s - m_new[:, :1])
            l_sc[hc] = jnp.broadcast_to(jnp.sum(p, axis=1, keepdims=True), (nq, LANES))
            acc_sc[hc] = jnp.dot(p.astype(BF16), v, preferred_element_type=F32)
        else:
            m_old = m_sc[hc]
            m_new = jnp.maximum(m_old, m_cur)
            alpha = jnp.exp2(m_old - m_new)
            p = jnp.exp2(s - m_new[:, :1])
            l_sc[hc] = alpha * l_sc[hc] + jnp.sum(p, axis=1, keepdims=True)
            acc_sc[hc] = _lane_tile(alpha, hw) * acc_sc[hc] + jnp.dot(p.astype(BF16), v, preferred_element_type=F32)
        m_sc[hc] = m_new

    @pl.when(kb == 0)
    def _():
        kpos = past + lax.broadcasted_iota(jnp.int32, (nq, nq), 1)
        for hc in range(2 * A_HEADS):
            h = hc // 2
            sweep(hc, kn_ref[:, hc * HEAD_DIM:(hc + 1) * HEAD_DIM], vn_ref[:, h * hw:(h + 1) * hw], kpos, True)

    kpos = kb * tk + lax.broadcasted_iota(jnp.int32, (nq, tk), 1)
    for h in range(A_HEADS):
        v = vc_ref[:, h].reshape(tk, hw).astype(BF16)
        for c in range(2):
            sweep(2 * h + c, kc_ref[:, 2 * h + c].reshape(tk, HEAD_DIM).astype(BF16), v, kpos, False)

    @pl.when(kb == pl.num_programs(1) - 1)
    def _():
        lam = _lambda(lam_ref, lam_init)
        for h in range(A_HEADS):
            o = (acc_sc[2 * h] / _lane_tile(l_sc[2 * h], hw)
                 - lam * (acc_sc[2 * h + 1] / _lane_tile(l_sc[2 * h + 1], hw)))
            o_ref[:, h * hw:(h + 1) * hw] = _head_rmsnorm(o, g_ref[...], 1.0 - lam_init).astype(o_ref.dtype)


def _diff_sample(q, kcache, vcache, knew, vnew, lam_p, g, lam_init, nq, tk=512):
    nb = kcache.shape[0]
    past = kcache.shape[1] * SUBLANES
    hw = 2 * HEAD_DIM
    row = pl.BlockSpec((nq, A_WIDTH), lambda b, j: (b, 0))
    return pl.pallas_call(
        functools.partial(_diff_sample_kernel, past=past, nq=nq, tk=tk, lam_init=lam_init),
        grid=(nb, past // tk),
        in_specs=[row,
                  pl.BlockSpec((None, tk // SUBLANES, 2 * A_HEADS, SUBLANES, HEAD_DIM), lambda b, j: (b, j, 0, 0, 0)),
                  pl.BlockSpec((None, tk // SUBLANES, A_HEADS, SUBLANES, hw), lambda b, j: (b, j, 0, 0, 0)),
                  row, row,
                  pl.BlockSpec((4, HEAD_DIM), lambda b, j: (0, 0)),
                  pl.BlockSpec((1, hw), lambda b, j: (0, 0))],
        out_specs=row,
        out_shape=jax.ShapeDtypeStruct((nb * nq, A_WIDTH), BF16),
        scratch_shapes=[pltpu.VMEM((2 * A_HEADS, nq, LANES), F32), pltpu.VMEM((2 * A_HEADS, nq, LANES), F32),
                        pltpu.VMEM((2 * A_HEADS, nq, hw), F32)],
        compiler_params=_params("parallel", "arbitrary"),
        name="diff_sample",
    )(q, kcache, vcache, knew, vnew, lam_p, g)


def _stick_sample_kernel(q_ref, kc_ref, vc_ref, kn_ref, vn_ref, g_ref, o_ref, carry_sc, acc_sc, *, nq, tk):
    kb = pl.program_id(1)

    @pl.when(kb == 0)
    def _():
        mask = lax.broadcasted_iota(jnp.int32, (nq, nq), 1) < lax.broadcasted_iota(jnp.int32, (nq, nq), 0)
        upper = _strict_upper(nq)
        for h in range(B_HEADS):
            sl = slice(h * HEAD_DIM, (h + 1) * HEAD_DIM)
            z = _dot_nt(q_ref[:, sl], kn_ref[:, sl])
            lb = _log_sigmoid(z)
            l1 = jnp.where(mask, lb - z, 0.0)
            tail = jnp.dot(l1.astype(BF16), upper, preferred_element_type=F32)
            a = jnp.where(mask, jnp.exp(lb + tail), 0.0)
            carry_sc[h] = jnp.broadcast_to(jnp.sum(l1, axis=1, keepdims=True), (nq, LANES))
            acc_sc[h] = jnp.dot(a.astype(BF16), vn_ref[:, sl], preferred_element_type=F32)

    @pl.when(jnp.max(carry_sc[...]) > UNDERFLOW_LOG)
    def _():
        cum = _cumsum_matrix(tk)
        for h in range(B_HEADS):
            k = kc_ref[:, h].reshape(tk, HEAD_DIM).astype(BF16)
            z = _dot_nt(q_ref[:, h * HEAD_DIM:(h + 1) * HEAD_DIM], k)
            lb = _log_sigmoid(z)
            sums = jnp.dot((lb - z).astype(BF16), cum, preferred_element_type=F32)
            a = jnp.exp(lb + sums[:, :tk] + _lane_tile(carry_sc[h], tk))
            carry_sc[h] += sums[:, tk:]
            v = vc_ref[:, h].reshape(tk, HEAD_DIM).astype(BF16)
            acc_sc[h] += jnp.dot(a.astype(BF16), v, preferred_element_type=F32)

    @pl.when(kb == pl.num_programs(1) - 1)
    def _():
        for h in range(B_HEADS):
            o_ref[:, h * HEAD_DIM:(h + 1) * HEAD_DIM] = _head_rmsnorm(acc_sc[h], g_ref[...]).astype(o_ref.dtype)


def _stick_sample(q, kcache, vcache, knew, vnew, g, nq, tk=256):
    nb = kcache.shape[0]
    nkb = kcache.shape[1] * SUBLANES // tk
    row = pl.BlockSpec((nq, B_WIDTH), lambda b, j: (b, 0))
    cache = pl.BlockSpec((None, tk // SUBLANES, B_HEADS, SUBLANES, HEAD_DIM), lambda b, j: (b, nkb - 1 - j, 0, 0, 0))
    return pl.pallas_call(
        functools.partial(_stick_sample_kernel, nq=nq, tk=tk),
        grid=(nb, nkb),
        in_specs=[row, cache, cache, row, row, pl.BlockSpec((1, HEAD_DIM), lambda b, j: (0, 0))],
        out_specs=row,
        out_shape=jax.ShapeDtypeStruct((nb * nq, B_WIDTH), BF16),
        scratch_shapes=[pltpu.VMEM((B_HEADS, nq, LANES), F32), pltpu.VMEM((B_HEADS, nq, HEAD_DIM), F32)],
        compiler_params=_params("parallel", "arbitrary"),
        name="stick_sample",
    )(q, kcache, vcache, knew, vnew, g)


def _by_sublane_tile(cache):
    nb, t, nh, w = cache.shape
    return cache.reshape(nb, t // SUBLANES, SUBLANES, nh, w).transpose(0, 1, 3, 2, 4)


def _project_all(x, g_attn, w_in, base, period, *, first_pass):
    m = x.shape[0]
    h = _rmsnorm(x, g_attn, BF16)
    tabs = _rope_tables(m, base, period)
    n = A_WIDTH
    groups = [(True, SCALE * LOG2E, False), (True, 1.0, True), (False, 1.0, True),
              (False, SCALE, False), (False, 1.0, True), (False, 1.0, True)]
    bf, f32, wb = [], [], []
    for g, (rope, scale, want_f32) in enumerate(groups):
        ep = functools.partial(_epilogue_inproj, rope=rope, scale=scale, want_f32=want_f32, want_bf16=True)
        dts = ([F32] if want_f32 else []) + [BF16]
        if first_pass:
            res = _matmul(h, w_in, ep, dts, tm=m, tn=1024, tk=1024, rows=tabs, n=n, col_block=g, emit_w=True,
                          name="inproj_first")
            wb.append(res[-1])
            res = res[:-1]
        else:
            res = _matmul(h, w_in[g], ep, dts, tm=1024, tn=512, rows=tabs, name="inproj")
        bf.append(res[-1])
        if want_f32:
            f32.append(res[0])
    return bf, f32, wb


def _finish(x, oa, ob, w_out, g_mlp, w_up, w_down, g_final, *, first_pass):
    m = x.shape[0]
    o = jnp.concatenate([oa, ob], axis=1)
    if first_pass:
        kw = dict(tm=m, tn=1024, tk=1024, emit_w=True)
        x1, wb_out = _matmul(o, w_out, _epilogue_residual, [F32], tiles=(x,), name="outproj_first", **kw)
        h2 = _rmsnorm(x1, g_mlp, BF16)
        u, wb_up = _matmul(h2, w_up, _epilogue_relu2, [BF16], name="mlp_up_first", **kw)
        x2, wb_down = _matmul(u, w_down, _epilogue_residual, [F32], tiles=(x1,), name="mlp_down_first", **kw)
        wb = (wb_out, wb_up, wb_down)
    else:
        (x1,) = _matmul(o, w_out, _epilogue_residual, [F32], tm=1024, tn=512, tiles=(x,), name="outproj")
        h2 = _rmsnorm(x1, g_mlp, BF16)
        (u,) = _matmul(h2, w_up, _epilogue_relu2, [BF16], tm=1024, tn=1024, name="mlp_up")
        (x2,) = _matmul(u, w_down, _epilogue_residual, [F32], tm=1024, tn=1024, tk=2048, tiles=(x1,), name="mlp_down")
        wb = None
    return _rmsnorm(x2, g_final, F32), wb


def kernel(x_prompt, x_sample, cache_a_k, cache_a_v, cache_b_k, cache_b_v, g_attn, w_in, lambda_q1, lambda_k1,
           lambda_q2, lambda_k2, g_subln, g_sb, w_out, g_mlp, w_up, w_down, g_final):
    depth = w_in.shape[0]
    assert depth == 1
    nb_p, seq, d = x_prompt.shape
    nb_s, dec, _ = x_sample.shape
    past = cache_a_k.shape[2]
    assert nb_p == 1 and d == D_MODEL

    l = 0
    lam_init = 0.8 - 0.6 * math.exp(-0.3 * l)
    lam_p = jnp.stack([lambda_q1[l], lambda_k1[l], lambda_q2[l], lambda_k2[l]]).astype(F32)
    g_sub = g_subln[l].reshape(1, 2 * HEAD_DIM)
    g_b = g_sb[l].reshape(1, HEAD_DIM)

    xs = x_sample.reshape(nb_s * dec, d)
    (aq, ak, av, bq, bk, bv), s32, wb_in = _project_all(xs, g_attn[l], w_in[l], past, dec, first_pass=True)
    oa = _diff_sample(aq, _by_sublane_tile(cache_a_k[l].reshape(nb_s, past, 2 * A_HEADS, HEAD_DIM)),
                      _by_sublane_tile(cache_a_v[l]), ak, av, lam_p, g_sub, lam_init, dec)
    ob = _stick_sample(bq, _by_sublane_tile(cache_b_k[l]), _by_sublane_tile(cache_b_v[l]), bk, bv, g_b, dec)
    y_sample, (wb_out, wb_up, wb_down) = _finish(xs, oa, ob, w_out[l], g_mlp[l], w_up[l], w_down[l], g_final,
                                                 first_pass=True)

    xp = x_prompt.reshape(seq, d)
    (aq, ak, av, bq, bk, bv), p32, _ = _project_all(xp, g_attn[l], wb_in, 0, seq, first_pass=False)
    oa = _diff_prompt(aq, ak, av, lam_p, g_sub, lam_init)
    ob = _stick_prompt(bq, bk, bv, g_b)
    y_prompt, _ = _finish(xp, oa, ob, wb_out, g_mlp[l], wb_up, wb_down, g_final, first_pass=False)

    return (y_prompt.reshape(nb_p, seq, d), y_sample.reshape(nb_s, dec, d),
            p32[0].reshape(1, nb_p, seq, A_HEADS, 2, HEAD_DIM), p32[1].reshape(1, nb_p, seq, A_HEADS, 2 * HEAD_DIM),
            p32[2].reshape(1, nb_p, seq, B_HEADS, HEAD_DIM), p32[3].reshape(1, nb_p, seq, B_HEADS, HEAD_DIM),
            s32[0].reshape(1, nb_s, dec, A_HEADS, 2, HEAD_DIM), s32[1].reshape(1, nb_s, dec, A_HEADS, 2 * HEAD_DIM),
            s32[2].reshape(1, nb_s, dec, B_HEADS, HEAD_DIM), s32[3].reshape(1, nb_s, dec, B_HEADS, HEAD_DIM))
```

```python
import functools
import math

import jax
import jax.numpy as jnp
from jax import lax
from jax.experimental import pallas as pl
from jax.experimental.pallas import tpu as pltpu

D_MODEL = 4096
HEAD_DIM = 128
A_HEADS = 8
B_HEADS = 16
A_WIDTH = A_HEADS * 2 * HEAD_DIM
B_WIDTH = B_HEADS * HEAD_DIM
ROT_DIM = HEAD_DIM // 4
ROPE_THETA = 500000.0
CHUNK = 64
D_FF = 4 * D_MODEL
EPS = 1e-6
NEG = -1e30
SCALE = HEAD_DIM ** -0.5
LOG2E = math.log2(math.e)
LANES = 128
SUBLANES = 8
UNDERFLOW_LOG = -110.0

VMEM_LIMIT_BYTES = 56 * 1024 * 1024

F32 = jnp.float32
BF16 = jnp.bfloat16


def _params(*sem):
    return pltpu.CompilerParams(dimension_semantics=sem, vmem_limit_bytes=VMEM_LIMIT_BYTES)


def _rmsnorm_kernel(x_ref, g_ref, o_ref):
    x = x_ref[...]
    ms = jnp.mean(x * x, axis=-1, keepdims=True)
    o_ref[...] = (x * lax.rsqrt(ms + EPS) * g_ref[...]).astype(o_ref.dtype)


def _rmsnorm(x, g, out_dtype, tm=256):
    m, d = x.shape
    return pl.pallas_call(
        _rmsnorm_kernel,
        grid=(m // tm,),
        in_specs=[pl.BlockSpec((tm, d), lambda i: (i, 0)), pl.BlockSpec((1, d), lambda i: (0, 0))],
        out_specs=pl.BlockSpec((tm, d), lambda i: (i, 0)),
        out_shape=jax.ShapeDtypeStruct((m, d), out_dtype),
        compiler_params=_params("arbitrary"),
        name="rmsnorm",
    )(x, g.reshape(1, d))


def _rope_table_kernel(inv_ref, c_ref, s1_ref, s2_ref, *, tm, base, period):
    half = ROT_DIM // 2
    row = lax.broadcasted_iota(jnp.int32, (tm, HEAD_DIM), 0) + pl.program_id(0) * tm
    lane = lax.broadcasted_iota(jnp.int32, (tm, HEAD_DIM), 1)
    pos = (base + row % period).astype(F32)
    ang = pos * inv_ref[...]
    cos = jnp.cos(ang)
    sin = jnp.sin(ang)
    c_ref[...] = jnp.where(lane < ROT_DIM, cos, 1.0)
    s1_ref[...] = jnp.where(lane < half, -sin, 0.0)
    s2_ref[...] = jnp.where((lane >= half) & (lane < ROT_DIM), sin, 0.0)


def _rope_tables(m, base, period, tm=256):
    half = ROT_DIM // 2
    inv = 1.0 / (ROPE_THETA ** (jnp.arange(0, ROT_DIM, 2, dtype=F32) / ROT_DIM))
    inv_lane = jnp.concatenate([inv, inv, jnp.zeros((HEAD_DIM - ROT_DIM,), F32)]).reshape(1, HEAD_DIM)
    assert inv.shape[0] == half
    spec = pl.BlockSpec((tm, HEAD_DIM), lambda i: (i, 0))
    shp = jax.ShapeDtypeStruct((m, HEAD_DIM), F32)
    return pl.pallas_call(
        functools.partial(_rope_table_kernel, tm=tm, base=base, period=period),
        grid=(m // tm,),
        in_specs=[pl.BlockSpec((1, HEAD_DIM), lambda i: (0, 0))],
        out_specs=[spec, spec, spec],
        out_shape=[shp, shp, shp],
        compiler_params=_params("arbitrary"),
        name="rope_tables",
    )(inv_lane)


def _epilogue_inproj(y, c0, extra, outs, *, rope, scale, want_f32, want_bf16):
    c_ref, s1_ref, s2_ref = extra
    half = ROT_DIM // 2
    for hh in range(y.shape[1] // HEAD_DIM):
        yh = y[:, hh * HEAD_DIM:(hh + 1) * HEAD_DIM]
        if rope:
            yh = (yh * c_ref[...]
                  + pltpu.roll(yh, HEAD_DIM - half, 1) * s1_ref[...]
                  + pltpu.roll(yh, half, 1) * s2_ref[...])
        sl = slice(c0 + hh * HEAD_DIM, c0 + (hh + 1) * HEAD_DIM)
        k = 0
        if want_f32:
            outs[k][:, sl] = yh
            k += 1
        if want_bf16:
            outs[k][:, sl] = (yh * scale).astype(BF16)


def _epilogue_residual(y, c0, extra, outs):
    (r_ref,), (o_ref,) = extra, outs
    sl = slice(c0, c0 + y.shape[1])
    o_ref[:, sl] = r_ref[:, sl] + y


def _epilogue_relu2(y, c0, extra, outs):
    (o_ref,) = outs
    u = jnp.maximum(y, 0.0)
    o_ref[:, c0:c0 + y.shape[1]] = (u * u).astype(o_ref.dtype)


def _mm_fullk_kernel(x_ref, w_ref, *rest, n_extra, epilogue, chunk):
    extra, outs = rest[:n_extra], rest[n_extra:]
    for c0 in range(0, w_ref.shape[1], chunk):
        y = jnp.dot(x_ref[...], w_ref[:, c0:c0 + chunk], preferred_element_type=F32)
        epilogue(y, c0, extra, outs)


def _mm_ksplit_kernel(x_ref, w_ref, *rest, n_extra, epilogue, chunk, emit_w):
    extra, outs, acc_ref = rest[:n_extra], rest[n_extra:-1], rest[-1]
    k = pl.program_id(2)

    @pl.when(k == 0)
    def _():
        acc_ref[...] = jnp.zeros_like(acc_ref)

    w = w_ref[...]
    if emit_w:
        outs, wb_ref = outs[:-1], outs[-1]
        w = w.astype(BF16)
        wb_ref[...] = w
    acc_ref[...] += jnp.dot(x_ref[...], w, preferred_element_type=F32)

    @pl.when(k == pl.num_programs(2) - 1)
    def _():
        for c0 in range(0, acc_ref.shape[1], chunk):
            epilogue(acc_ref[:, c0:c0 + chunk], c0, extra, outs)


def _matmul(x, w, epilogue, out_dtypes, *, tm, tn, tk=None, rows=(), tiles=(), n=None, col_block=0, emit_w=False,
            chunk=256, name):
    m, kdim = x.shape
    n = w.shape[1] if n is None else n
    off = col_block * (n // tn)
    out_shape = [jax.ShapeDtypeStruct((m, n), dt) for dt in out_dtypes]
    if tk is None:
        grid = (m // tm, n // tn)
        in_specs = ([pl.BlockSpec((tm, kdim), lambda i, j: (i, 0)), pl.BlockSpec((kdim, tn), lambda i, j: (0, j + off))]
                    + [pl.BlockSpec((tm, LANES), lambda i, j: (i, 0)) for _ in rows]
                    + [pl.BlockSpec((tm, tn), lambda i, j: (i, j)) for _ in tiles])
        out_specs = [pl.BlockSpec((tm, tn), lambda i, j: (i, j)) for _ in out_dtypes]
        body = functools.partial(_mm_fullk_kernel, n_extra=len(rows) + len(tiles), epilogue=epilogue, chunk=chunk)
        scratch, sem = [], ("parallel", "arbitrary")
        assert not emit_w
    else:
        grid = (m // tm, n // tn, kdim // tk)
        in_specs = ([pl.BlockSpec((tm, tk), lambda i, j, k: (i, k)), pl.BlockSpec((tk, tn), lambda i, j, k: (k, j + off))]
                    + [pl.BlockSpec((tm, LANES), lambda i, j, k: (i, 0)) for _ in rows]
                    + [pl.BlockSpec((tm, tn), lambda i, j, k: (i, j)) for _ in tiles])
        out_specs = [pl.BlockSpec((tm, tn), lambda i, j, k: (i, j)) for _ in out_dtypes]
        if emit_w:
            assert m == tm and w.dtype == F32
            out_specs.append(pl.BlockSpec((tk, tn), lambda i, j, k: (k, j)))
            out_shape.append(jax.ShapeDtypeStruct((kdim, n), BF16))
        body = functools.partial(_mm_ksplit_kernel, n_extra=len(rows) + len(tiles), epilogue=epilogue, chunk=chunk,
                                 emit_w=emit_w)
        scratch, sem = [pltpu.VMEM((tm, tn), F32)], ("parallel", "parallel", "arbitrary")
    return pl.pallas_call(body, grid=grid, in_specs=in_specs, out_specs=out_specs, out_shape=out_shape,
                          scratch_shapes=scratch, compiler_params=_params(*sem), name=name)(x, w, *rows, *tiles)


def _lambda(lam_ref, lam_init):
    lp = lam_ref[...]
    s1 = jnp.sum(lp[0:1] * lp[1:2], axis=-1, keepdims=True)
    s2 = jnp.sum(lp[2:3] * lp[3:4], axis=-1, keepdims=True)
    return jnp.exp(s1) - jnp.exp(s2) + lam_init


def _head_rmsnorm(o, g, mult=None):
    y = o * lax.rsqrt(jnp.mean(o * o, axis=-1, keepdims=True) + EPS) * g
    if mult is not None:
        y = y * mult
    return y


def _dot_nt(a, b):
    return lax.dot_general(a, b, (((1,), (1,)), ((), ())), preferred_element_type=F32)


def _log_sigmoid(z):
    return jnp.minimum(z, 0.0) - jnp.log(1.0 + jnp.exp(-jnp.abs(z)))


def _strict_upper(n):
    r = lax.broadcasted_iota(jnp.int32, (n, n), 0)
    c = lax.broadcasted_iota(jnp.int32, (n, n), 1)
    return (r > c).astype(BF16)


def _lane_tile(x, width):
    return jnp.tile(x, (1, width // x.shape[1]))


def _lane_fold(x):
    out = x[:, :LANES]
    for i in range(1, x.shape[1] // LANES):
        out = out + x[:, i * LANES:(i + 1) * LANES]
    return out


def _diff_prompt_kernel(q_ref, k_ref, v_ref, lam_ref, g_ref, o_ref, m_sc, l_sc, acc_sc, *, tq, tk, lam_init):
    qi = pl.program_id(1)
    hw = 2 * HEAD_DIM

    def block(kstart, width, first):
        kb = k_ref[pl.ds(kstart, width), :]
        vb = v_ref[pl.ds(kstart, width), :]
        for c in range(2):
            sl = slice(c * HEAD_DIM, (c + 1) * HEAD_DIM)
            s = _dot_nt(q_ref[:, sl], kb[:, sl])
            if first:
                qc = lax.broadcasted_iota(jnp.int32, (tq, width), 0) // CHUNK
                kc = lax.broadcasted_iota(jnp.int32, (tq, width), 1) // CHUNK
                s = jnp.where(kc <= qc, s, NEG)
                m_new = jnp.broadcast_to(jnp.max(s, axis=1, keepdims=True), (tq, LANES))
                p = jnp.exp2(s - _lane_tile(m_new, width))
                l_sc[c] = _lane_fold(p)
                acc_sc[c] = jnp.dot(p.astype(BF16), vb, preferred_element_type=F32)
            else:
                m_old = m_sc[c]
                m_new = jnp.maximum(m_old, jnp.max(s, axis=1, keepdims=True))
                alpha = jnp.exp2(m_old - m_new)
                p = jnp.exp2(s - _lane_tile(m_new, width))
                l_sc[c] = alpha * l_sc[c] + _lane_fold(p)
                acc_sc[c] = _lane_tile(alpha, hw) * acc_sc[c] + jnp.dot(p.astype(BF16), vb,
                                                                         preferred_element_type=F32)
            m_sc[c] = m_new

    block(pl.multiple_of(qi * tq, tq), tq, True)

    def body(j, carry):
        block(pl.multiple_of(j * tk, tk), tk, False)
        return carry

    nwide = (qi * tq) // tk
    lax.fori_loop(0, nwide, body, 0)
    if tk != tq:
        @pl.when(nwide * tk < qi * tq)
        def _():
            block(pl.multiple_of(nwide * tk, tq), tq, False)

    lam = _lambda(lam_ref, lam_init)
    l0 = jnp.sum(l_sc[0], axis=1, keepdims=True)
    l1 = jnp.sum(l_sc[1], axis=1, keepdims=True)
    o = acc_sc[0] / l0 - lam * (acc_sc[1] / l1)
    o_ref[...] = _head_rmsnorm(o, g_ref[...], 1.0 - lam_init).astype(o_ref.dtype)


def _diff_prompt(q, k, v, lam_p, g, lam_init, tq=512, tk=1024):
    t = q.shape[0]
    assert tk in (tq, 2 * tq)
    hw = 2 * HEAD_DIM
    return pl.pallas_call(
        functools.partial(_diff_prompt_kernel, tq=tq, tk=tk, lam_init=lam_init),
        grid=(A_HEADS, t // tq),
        in_specs=[pl.BlockSpec((tq, hw), lambda h, i: (i, h)),
                  pl.BlockSpec((t, hw), lambda h, i: (0, h)),
                  pl.BlockSpec((t, hw), lambda h, i: (0, h)),
                  pl.BlockSpec((4, HEAD_DIM), lambda h, i: (0, 0)),
                  pl.BlockSpec((1, hw), lambda h, i: (0, 0))],
        out_specs=pl.BlockSpec((tq, hw), lambda h, i: (i, h)),
        out_shape=jax.ShapeDtypeStruct((t, A_WIDTH), BF16),
        scratch_shapes=[pltpu.VMEM((2, tq, LANES), F32), pltpu.VMEM((2, tq, LANES), F32),
                        pltpu.VMEM((2, tq, hw), F32)],
        compiler_params=_params("parallel", "arbitrary"),
        name="diff_prompt",
    )(q, k, v, lam_p, g)


def _cumsum_matrix(n):
    r = lax.broadcasted_iota(jnp.int32, (n, n + LANES), 0)
    c = lax.broadcasted_iota(jnp.int32, (n, n + LANES), 1)
    return ((r > c) | (c >= n)).astype(BF16)


def _stick_prompt_kernel(q_ref, k_ref, v_ref, g_ref, o_ref, carry_sc, acc_sc, *, tq, nh):
    qi = pl.program_id(1)
    cum = _cumsum_matrix(tq)

    def block(kstart, first):
        for hh in range(nh):
            sl = slice(hh * HEAD_DIM, (hh + 1) * HEAD_DIM)
            kb = k_ref[pl.ds(kstart, tq), sl]
            vb = v_ref[pl.ds(kstart, tq), sl]
            z = _dot_nt(q_ref[:, sl], kb)
            lb = _log_sigmoid(z)
            l1 = lb - z
            if first:
                mask = (lax.broadcasted_iota(jnp.int32, (tq, tq), 1)
                        < lax.broadcasted_iota(jnp.int32, (tq, tq), 0))
                l1 = jnp.where(mask, l1, 0.0)
            sums = jnp.dot(l1.astype(BF16), cum, preferred_element_type=F32)
            if first:
                a = jnp.where(mask, jnp.exp(lb + sums[:, :tq]), 0.0)
                carry_sc[hh] = sums[:, tq:]
                acc_sc[hh] = jnp.dot(a.astype(BF16), vb, preferred_element_type=F32)
            else:
                a = jnp.exp(lb + sums[:, :tq] + _lane_tile(carry_sc[hh], tq))
                carry_sc[hh] += sums[:, tq:]
                acc_sc[hh] += jnp.dot(a.astype(BF16), vb, preferred_element_type=F32)

    block(pl.multiple_of(qi * tq, tq), True)

    def cond(st):
        j, cmax = st
        return (j >= 0) & (cmax > UNDERFLOW_LOG)

    def body(st):
        j, _ = st
        block(pl.multiple_of(j * tq, tq), False)
        return j - 1, jnp.max(carry_sc[...])

    lax.while_loop(cond, body, (qi - 1, jnp.max(carry_sc[...])))
    for hh in range(nh):
        sl = slice(hh * HEAD_DIM, (hh + 1) * HEAD_DIM)
        o_ref[:, sl] = _head_rmsnorm(acc_sc[hh], g_ref[...]).astype(o_ref.dtype)


def _stick_prompt(q, k, v, g, tq=256, nh=4):
    t = q.shape[0]
    w = nh * HEAD_DIM
    return pl.pallas_call(
        functools.partial(_stick_prompt_kernel, tq=tq, nh=nh),
        grid=(B_HEADS // nh, t // tq),
        in_specs=[pl.BlockSpec((tq, w), lambda h, i: (i, h)),
                  pl.BlockSpec((t, w), lambda h, i: (0, h)),
                  pl.BlockSpec((t, w), lambda h, i: (0, h)),
                  pl.BlockSpec((1, HEAD_DIM), lambda h, i: (0, 0))],
        out_specs=pl.BlockSpec((tq, w), lambda h, i: (i, h)),
        out_shape=jax.ShapeDtypeStruct((t, B_WIDTH), BF16),
        scratch_shapes=[pltpu.VMEM((nh, tq, LANES), F32), pltpu.VMEM((nh, tq, HEAD_DIM), F32)],
        compiler_params=_params("parallel", "arbitrary"),
        name="stick_prompt",
    )(q, k, v, g)


def _diff_sample_kernel(q_ref, kn_ref, vn_ref, lam_ref, g_ref, kc_hbm, vc_hbm, o_ref, kbuf, vbuf, sem, m_sc, l_sc, acc_sc,
                        *, nb, past, nq, tk, lam_init):
    hw = 2 * HEAD_DIM
    nkb = past // tk
    nsteps = nb * nkb
    qchunk = (past + lax.broadcasted_iota(jnp.int32, (nq, 1), 0)) // CHUNK

    def copies(s, slot):
        b, j = s // nkb, s % nkb
        frames = pl.ds(pl.multiple_of(j * tk, tk), tk)
        ks = [pltpu.make_async_copy(kc_hbm.at[b, frames, hc], kbuf.at[slot, hc], sem.at[slot, hc])
              for hc in range(2 * A_HEADS)]
        vs = [pltpu.make_async_copy(vc_hbm.at[b, frames, h], vbuf.at[slot, h], sem.at[slot, 2 * A_HEADS + h])
              for h in range(A_HEADS)]
        return ks + vs

    def sweep(rows, hc, k, v, kpos, first):
        s = _dot_nt(q_ref[rows, hc * HEAD_DIM:(hc + 1) * HEAD_DIM], k)
        s = jnp.where(kpos // CHUNK <= qchunk, s, NEG)
        m_cur = jnp.broadcast_to(jnp.max(s, axis=1, keepdims=True), (nq, LANES))
        if first:
            m_new = m_cur
            p = jnp.exp2(s - m_new[:, :1])
            l_sc[hc] = jnp.broadcast_to(jnp.sum(p, axis=1, keepdims=True), (nq, LANES))
            acc_sc[hc] = jnp.dot(p.astype(BF16), v, preferred_element_type=F32)
        else:
            m_old = m_sc[hc]
            m_new = jnp.maximum(m_old, m_cur)
            alpha = jnp.exp2(m_old - m_new)
            p = jnp.exp2(s - m_new[:, :1])
            l_sc[hc] = alpha * l_sc[hc] + jnp.sum(p, axis=1, keepdims=True)
            acc_sc[hc] = _lane_tile(alpha, hw) * acc_sc[hc] + jnp.dot(p.astype(BF16), v, preferred_element_type=F32)
        m_sc[hc] = m_new

    for cp in copies(0, 0):
        cp.start()

    def step(s, carry):
        slot = s % 2
        b, j = s // nkb, s % nkb
        rows = pl.ds(pl.multiple_of(b * nq, nq), nq)
        for cp in copies(s, slot):
            cp.wait()

        @pl.when(s + 1 < nsteps)
        def _():
            for cp in copies(s + 1, 1 - slot):
                cp.start()

        @pl.when(j == 0)
        def _():
            kpos = past + lax.broadcasted_iota(jnp.int32, (nq, nq), 1)
            for hc in range(2 * A_HEADS):
                h = hc // 2
                sweep(rows, hc, kn_ref[rows, hc * HEAD_DIM:(hc + 1) * HEAD_DIM], vn_ref[rows, h * hw:(h + 1) * hw],
                      kpos, True)

        kpos = j * tk + lax.broadcasted_iota(jnp.int32, (nq, tk), 1)
        for h in range(A_HEADS):
            v = vbuf[slot, h].astype(BF16)
            for c in range(2):
                sweep(rows, 2 * h + c, kbuf[slot, 2 * h + c].astype(BF16), v, kpos, False)

        @pl.when(j == nkb - 1)
        def _():
            lam = _lambda(lam_ref, lam_init)
            for h in range(A_HEADS):
                o = (acc_sc[2 * h] / _lane_tile(l_sc[2 * h], hw)
                     - lam * (acc_sc[2 * h + 1] / _lane_tile(l_sc[2 * h + 1], hw)))
                o_ref[rows, h * hw:(h + 1) * hw] = _head_rmsnorm(o, g_ref[...], 1.0 - lam_init).astype(o_ref.dtype)

        return carry

    lax.fori_loop(0, nsteps, step, 0)


def _diff_sample(q, kcache, vcache, knew, vnew, lam_p, g, lam_init, nq, tk=512):
    nb, past = kcache.shape[:2]
    hw = 2 * HEAD_DIM
    vmem = pl.BlockSpec(memory_space=pltpu.VMEM)
    hbm = pl.BlockSpec(memory_space=pl.ANY)
    return pl.pallas_call(
        functools.partial(_diff_sample_kernel, nb=nb, past=past, nq=nq, tk=tk, lam_init=lam_init),
        in_specs=[vmem, vmem, vmem, vmem, vmem, hbm, hbm],
        out_specs=vmem,
        out_shape=jax.ShapeDtypeStruct((nb * nq, A_WIDTH), BF16),
        scratch_shapes=[pltpu.VMEM((2, 2 * A_HEADS, tk, HEAD_DIM), F32), pltpu.VMEM((2, A_HEADS, tk, hw), F32),
                        pltpu.SemaphoreType.DMA((2, 3 * A_HEADS)),
                        pltpu.VMEM((2 * A_HEADS, nq, LANES), F32), pltpu.VMEM((2 * A_HEADS, nq, LANES), F32),
                        pltpu.VMEM((2 * A_HEADS, nq, hw), F32)],
        compiler_params=pltpu.CompilerParams(vmem_limit_bytes=VMEM_LIMIT_BYTES),
        name="diff_sample",
    )(q, knew, vnew, lam_p, g, kcache, vcache)


def _stick_sample_kernel(q_ref, kn_ref, vn_ref, g_ref, kc_hbm, vc_hbm, o_ref, kbuf, vbuf, sem, carry_sc, acc_sc,
                         *, nb, nq, tk, nkb):
    cum = _cumsum_matrix(tk)
    mask = lax.broadcasted_iota(jnp.int32, (nq, nq), 1) < lax.broadcasted_iota(jnp.int32, (nq, nq), 0)
    upper = _strict_upper(nq)

    def copies(b, j, slot):
        frames = pl.ds(pl.multiple_of(j * tk, tk), tk)
        ks = [pltpu.make_async_copy(kc_hbm.at[b, frames, h], kbuf.at[slot, h], sem.at[slot, h])
              for h in range(B_HEADS)]
        vs = [pltpu.make_async_copy(vc_hbm.at[b, frames, h], vbuf.at[slot, h], sem.at[slot, B_HEADS + h])
              for h in range(B_HEADS)]
        return ks + vs

    for cp in copies(0, nkb - 1, 0):
        cp.start()

    def stream(b, carry):
        rows = pl.ds(pl.multiple_of(b * nq, nq), nq)
        for h in range(B_HEADS):
            sl = slice(h * HEAD_DIM, (h + 1) * HEAD_DIM)
            z = _dot_nt(q_ref[rows, sl], kn_ref[rows, sl])
            lb = _log_sigmoid(z)
            l1 = jnp.where(mask, lb - z, 0.0)
            tail = jnp.dot(l1.astype(BF16), upper, preferred_element_type=F32)
            a = jnp.where(mask, jnp.exp(lb + tail), 0.0)
            carry_sc[h] = jnp.broadcast_to(jnp.sum(l1, axis=1, keepdims=True), (nq, LANES))
            acc_sc[h] = jnp.dot(a.astype(BF16), vn_ref[rows, sl], preferred_element_type=F32)

        def cond(st):
            j, cmax = st
            return (j >= 0) & (cmax > UNDERFLOW_LOG)

        def body(st):
            j, _ = st
            slot = (nkb - 1 - j) % 2
            for cp in copies(b, j, slot):
                cp.wait()

            @pl.when(j > 0)
            def _():
                for cp in copies(b, j - 1, 1 - slot):
                    cp.start()

            for h in range(B_HEADS):
                z = _dot_nt(q_ref[rows, h * HEAD_DIM:(h + 1) * HEAD_DIM], kbuf[slot, h].astype(BF16))
                lb = _log_sigmoid(z)
                sums = jnp.dot((lb - z).astype(BF16), cum, preferred_element_type=F32)
                a = jnp.exp(lb + sums[:, :tk] + _lane_tile(carry_sc[h], tk))
                carry_sc[h] += sums[:, tk:]
                acc_sc[h] += jnp.dot(a.astype(BF16), vbuf[slot, h].astype(BF16), preferred_element_type=F32)
            return j - 1, jnp.max(carry_sc[...])

        j_end, _ = lax.while_loop(cond, body, (nkb - 1, jnp.max(carry_sc[...])))

        @pl.when(j_end >= 0)
        def _():
            for cp in copies(b, j_end, (nkb - 1 - j_end) % 2):
                cp.wait()

        @pl.when(b + 1 < nb)
        def _():
            for cp in copies(b + 1, nkb - 1, 0):
                cp.start()

        for h in range(B_HEADS):
            o_ref[rows, h * HEAD_DIM:(h + 1) * HEAD_DIM] = _head_rmsnorm(acc_sc[h], g_ref[...]).astype(o_ref.dtype)
        return carry

    lax.fori_loop(0, nb, stream, 0)


def _stick_sample(q, kcache, vcache, knew, vnew, g, nq, tk=256):
    nb, past = kcache.shape[:2]
    vmem = pl.BlockSpec(memory_space=pltpu.VMEM)
    hbm = pl.BlockSpec(memory_space=pl.ANY)
    return pl.pallas_call(
        functools.partial(_stick_sample_kernel, nb=nb, nq=nq, tk=tk, nkb=past // tk),
        in_specs=[vmem, vmem, vmem, vmem, hbm, hbm],
        out_specs=vmem,
        out_shape=jax.ShapeDtypeStruct((nb * nq, B_WIDTH), BF16),
        scratch_shapes=[pltpu.VMEM((2, B_HEADS, tk, HEAD_DIM), F32), pltpu.VMEM((2, B_HEADS, tk, HEAD_DIM), F32),
                        pltpu.SemaphoreType.DMA((2, 2 * B_HEADS)),
                        pltpu.VMEM((B_HEADS, nq, LANES), F32), pltpu.VMEM((B_HEADS, nq, HEAD_DIM), F32)],
        compiler_params=pltpu.CompilerParams(vmem_limit_bytes=VMEM_LIMIT_BYTES),
        name="stick_sample",
    )(q, knew, vnew, g, kcache, vcache)


def _project_all(x, g_attn, w_in, base, period, *, first_pass):
    m = x.shape[0]
    h = _rmsnorm(x, g_attn, BF16)
    tabs = _rope_tables(m, base, period)
    n = A_WIDTH
    groups = [(True, SCALE * LOG2E, False), (True, 1.0, True), (False, 1.0, True),
              (False, SCALE, False), (False, 1.0, True), (False, 1.0, True)]
    bf, f32, wb = [], [], []
    for g, (rope, scale, want_f32) in enumerate(groups):
        ep = functools.partial(_epilogue_inproj, rope=rope, scale=scale, want_f32=want_f32, want_bf16=True)
        dts = ([F32] if want_f32 else []) + [BF16]
        if first_pass:
            res = _matmul(h, w_in, ep, dts, tm=m, tn=1024, tk=1024, rows=tabs, n=n, col_block=g, emit_w=True,
                          name="inproj_first")
            wb.append(res[-1])
            res = res[:-1]
        else:
            res = _matmul(h, w_in[g], ep, dts, tm=1024, tn=512, rows=tabs, name="inproj")
        bf.append(res[-1])
        if want_f32:
            f32.append(res[0])
    return bf, f32, wb


def _finish(x, oa, ob, w_out, g_mlp, w_up, w_down, g_final, *, first_pass):
    m = x.shape[0]
    o = jnp.concatenate([oa, ob], axis=1)
    if first_pass:
        kw = dict(tm=m, tn=1024, tk=1024, emit_w=True)
        x1, wb_out = _matmul(o, w_out, _epilogue_residual, [F32], tiles=(x,), name="outproj_first", **kw)
        h2 = _rmsnorm(x1, g_mlp, BF16)
        u, wb_up = _matmul(h2, w_up, _epilogue_relu2, [BF16], name="mlp_up_first", **kw)
        x2, wb_down = _matmul(u, w_down, _epilogue_residual, [F32], tiles=(x1,), name="mlp_down_first", **kw)
        wb = (wb_out, wb_up, wb_down)
    else:
        (x1,) = _matmul(o, w_out, _epilogue_residual, [F32], tm=1024, tn=512, tiles=(x,), name="outproj")
        h2 = _rmsnorm(x1, g_mlp, BF16)
        (u,) = _matmul(h2, w_up, _epilogue_relu2, [BF16], tm=1024, tn=1024, name="mlp_up")
        (x2,) = _matmul(u, w_down, _epilogue_residual, [F32], tm=1024, tn=1024, tk=2048, tiles=(x1,), name="mlp_down")
        wb = None
    return _rmsnorm(x2, g_final, F32), wb


def kernel(x_prompt, x_sample, cache_a_k, cache_a_v, cache_b_k, cache_b_v, g_attn, w_in, lambda_q1, lambda_k1,
           lambda_q2, lambda_k2, g_subln, g_sb, w_out, g_mlp, w_up, w_down, g_final):
    depth = w_in.shape[0]
    assert depth == 1
    nb_p, seq, d = x_prompt.shape
    nb_s, dec, _ = x_sample.shape
    past = cache_a_k.shape[2]
    assert nb_p == 1 and d == D_MODEL

    l = 0
    lam_init = 0.8 - 0.6 * math.exp(-0.3 * l)
    lam_p = jnp.stack([lambda_q1[l], lambda_k1[l], lambda_q2[l], lambda_k2[l]]).astype(F32)
    g_sub = g_subln[l].reshape(1, 2 * HEAD_DIM)
    g_b = g_sb[l].reshape(1, HEAD_DIM)

    xs = x_sample.reshape(nb_s * dec, d)
    (aq, ak, av, bq, bk, bv), s32, wb_in = _project_all(xs, g_attn[l], w_in[l], past, dec, first_pass=True)
    oa = _diff_sample(aq, cache_a_k[l].reshape(nb_s, past, 2 * A_HEADS, HEAD_DIM), cache_a_v[l], ak, av, lam_p, g_sub,
                      lam_init, dec)
    ob = _stick_sample(bq, cache_b_k[l], cache_b_v[l], bk, bv, g_b, dec)
    y_sample, (wb_out, wb_up, wb_down) = _finish(xs, oa, ob, w_out[l], g_mlp[l], w_up[l], w_down[l], g_final,
                                                 first_pass=True)

    xp = x_prompt.reshape(seq, d)
    (aq, ak, av, bq, bk, bv), p32, _ = _project_all(xp, g_attn[l], wb_in, 0, seq, first_pass=False)
    oa = _diff_prompt(aq, ak, av, lam_p, g_sub, lam_init)
    ob = _stick_prompt(bq, bk, bv, g_b)
    y_prompt, _ = _finish(xp, oa, ob, wb_out, g_mlp[l], wb_up, wb_down, g_final, first_pass=False)

    return (y_prompt.reshape(nb_p, seq, d), y_sample.reshape(nb_s, dec, d),
            p32[0].reshape(1, nb_p, seq, A_HEADS, 2, HEAD_DIM), p32[1].reshape(1, nb_p, seq, A_HEADS, 2 * HEAD_DIM),
            p32[2].reshape(1, nb_p, seq, B_HEADS, HEAD_DIM), p32[3].reshape(1, nb_p, seq, B_HEADS, HEAD_DIM),
            s32[0].reshape(1, nb_s, dec, A_HEADS, 2, HEAD_DIM), s32[1].reshape(1, nb_s, dec, A_HEADS, 2 * HEAD_DIM),
            s32[2].reshape(1, nb_s, dec, B_HEADS, HEAD_DIM), s32[3].reshape(1, nb_s, dec, B_HEADS, HEAD_DIM))
```

```python
import functools
import math

import jax
import jax.numpy as jnp
from jax import lax
from jax.experimental import pallas as pl
from jax.experimental.pallas import tpu as pltpu

D_MODEL = 4096
HEAD_DIM = 128
A_HEADS = 8
B_HEADS = 16
A_WIDTH = A_HEADS * 2 * HEAD_DIM
B_WIDTH = B_HEADS * HEAD_DIM
ROT_DIM = HEAD_DIM // 4
ROPE_THETA = 500000.0
CHUNK = 64
D_FF = 4 * D_MODEL
EPS = 1e-6
NEG = -1e30
SCALE = HEAD_DIM ** -0.5
LOG2E = math.log2(math.e)
LANES = 128
SUBLANES = 8
UNDERFLOW_LOG = -110.0

VMEM_LIMIT_BYTES = 56 * 1024 * 1024

F32 = jnp.float32
BF16 = jnp.bfloat16


def _params(*sem):
    return pltpu.CompilerParams(dimension_semantics=sem, vmem_limit_bytes=VMEM_LIMIT_BYTES)


def _rmsnorm_kernel(x_ref, g_ref, o_ref):
    x = x_ref[...]
    ms = jnp.mean(x * x, axis=-1, keepdims=True)
    o_ref[...] = (x * lax.rsqrt(ms + EPS) * g_ref[...]).astype(o_ref.dtype)


def _rmsnorm(x, g, out_dtype, tm=256):
    m, d = x.shape
    return pl.pallas_call(
        _rmsnorm_kernel,
        grid=(m // tm,),
        in_specs=[pl.BlockSpec((tm, d), lambda i: (i, 0)), pl.BlockSpec((1, d), lambda i: (0, 0))],
        out_specs=pl.BlockSpec((tm, d), lambda i: (i, 0)),
        out_shape=jax.ShapeDtypeStruct((m, d), out_dtype),
        compiler_params=_params("arbitrary"),
        name="rmsnorm",
    )(x, g.reshape(1, d))


def _rope_table_kernel(inv_ref, c_ref, s1_ref, s2_ref, *, tm, base, period):
    half = ROT_DIM // 2
    row = lax.broadcasted_iota(jnp.int32, (tm, HEAD_DIM), 0) + pl.program_id(0) * tm
    lane = lax.broadcasted_iota(jnp.int32, (tm, HEAD_DIM), 1)
    pos = (base + row % period).astype(F32)
    ang = pos * inv_ref[...]
    cos = jnp.cos(ang)
    sin = jnp.sin(ang)
    c_ref[...] = jnp.where(lane < ROT_DIM, cos, 1.0)
    s1_ref[...] = jnp.where(lane < half, -sin, 0.0)
    s2_ref[...] = jnp.where((lane >= half) & (lane < ROT_DIM), sin, 0.0)


def _rope_tables(m, base, period, tm=256):
    half = ROT_DIM // 2
    inv = 1.0 / (ROPE_THETA ** (jnp.arange(0, ROT_DIM, 2, dtype=F32) / ROT_DIM))
    inv_lane = jnp.concatenate([inv, inv, jnp.zeros((HEAD_DIM - ROT_DIM,), F32)]).reshape(1, HEAD_DIM)
    assert inv.shape[0] == half
    spec = pl.BlockSpec((tm, HEAD_DIM), lambda i: (i, 0))
    shp = jax.ShapeDtypeStruct((m, HEAD_DIM), F32)
    return pl.pallas_call(
        functools.partial(_rope_table_kernel, tm=tm, base=base, period=period),
        grid=(m // tm,),
        in_specs=[pl.BlockSpec((1, HEAD_DIM), lambda i: (0, 0))],
        out_specs=[spec, spec, spec],
        out_shape=[shp, shp, shp],
        compiler_params=_params("arbitrary"),
        name="rope_tables",
    )(inv_lane)


def _epilogue_inproj(y, c0, extra, outs, *, rope, scale, want_f32, want_bf16, rows_by_head=None):
    c_ref, s1_ref, s2_ref = extra
    half = ROT_DIM // 2
    for hh in range(y.shape[1] // HEAD_DIM):
        yh = y[:, hh * HEAD_DIM:(hh + 1) * HEAD_DIM]
        if rope:
            yh = (yh * c_ref[...]
                  + pltpu.roll(yh, HEAD_DIM - half, 1) * s1_ref[...]
                  + pltpu.roll(yh, half, 1) * s2_ref[...])
        sl = slice(c0 + hh * HEAD_DIM, c0 + (hh + 1) * HEAD_DIM)
        k = 0
        if want_f32:
            if rows_by_head is None:
                outs[k][:, sl] = yh
            else:
                tn, heads = rows_by_head
                head = (pl.program_id(1) * tn + c0) // HEAD_DIM + hh
                outs[k][pl.ds(head, yh.shape[0], stride=heads), :] = yh
            k += 1
        if want_bf16:
            outs[k][:, sl] = (yh * scale).astype(BF16)


def _epilogue_inproj_all(y, c0, extra, outs, *, tn):
    g = (pl.program_id(1) * tn + c0) // A_WIDTH
    scale = jnp.where(g == 0, SCALE * LOG2E, jnp.where(g == 3, SCALE, 1.0))

    @pl.when(g < 2)
    def _():
        _epilogue_inproj(y, c0, extra, outs, rope=True, scale=scale, want_f32=True, want_bf16=True)

    @pl.when(g >= 2)
    def _():
        _epilogue_inproj(y, c0, extra, outs, rope=False, scale=scale, want_f32=True, want_bf16=True)


def _epilogue_residual(y, c0, extra, outs):
    (r_ref,), (o_ref,) = extra, outs
    sl = slice(c0, c0 + y.shape[1])
    o_ref[:, sl] = r_ref[:, sl] + y


def _epilogue_relu2(y, c0, extra, outs):
    (o_ref,) = outs
    u = jnp.maximum(y, 0.0)
    o_ref[:, c0:c0 + y.shape[1]] = (u * u).astype(o_ref.dtype)


def _mm_fullk_kernel(x_ref, w_ref, *rest, n_extra, epilogue, chunk):
    extra, outs = rest[:n_extra], rest[n_extra:]
    for c0 in range(0, w_ref.shape[1], chunk):
        y = jnp.dot(x_ref[...], w_ref[:, c0:c0 + chunk], preferred_element_type=F32)
        epilogue(y, c0, extra, outs)


def _mm_ksplit_kernel(x_ref, w_ref, *rest, n_extra, epilogue, chunk, emit_w):
    extra, outs, acc_ref = rest[:n_extra], rest[n_extra:-1], rest[-1]
    k = pl.program_id(2)

    @pl.when(k == 0)
    def _():
        acc_ref[...] = jnp.zeros_like(acc_ref)

    w = w_ref[...]
    if emit_w:
        outs, wb_ref = outs[:-1], outs[-1]
        w = w.astype(BF16)
        wb_ref[...] = w
    acc_ref[...] += jnp.dot(x_ref[...], w, preferred_element_type=F32)

    @pl.when(k == pl.num_programs(2) - 1)
    def _():
        for c0 in range(0, acc_ref.shape[1], chunk):
            epilogue(acc_ref[:, c0:c0 + chunk], c0, extra, outs)


def _matmul(x, w, epilogue, out_dtypes, *, tm, tn, tk=None, rows=(), tiles=(), n=None, col_block=0, emit_w=False,
            chunk=256, by_head=(), name):
    m, kdim = x.shape
    n = w.shape[1] if n is None else n
    off = col_block * (n // tn)
    out_shape = [jax.ShapeDtypeStruct((m, n), dt) for dt in out_dtypes]
    if tk is None:
        grid = (m // tm, n // tn)
        in_specs = ([pl.BlockSpec((tm, kdim), lambda i, j: (i, 0)), pl.BlockSpec((kdim, tn), lambda i, j: (0, j + off))]
                    + [pl.BlockSpec((tm, LANES), lambda i, j: (i, 0)) for _ in rows]
                    + [pl.BlockSpec((tm, tn), lambda i, j: (i, j)) for _ in tiles])
        out_specs = [pl.BlockSpec((tm, tn), lambda i, j: (i, j)) for _ in out_dtypes]
        heads = n // HEAD_DIM
        for o in by_head:
            out_specs[o] = pl.BlockSpec((tm * heads, HEAD_DIM), lambda i, j: (i, 0))
            out_shape[o] = jax.ShapeDtypeStruct((m * heads, HEAD_DIM), out_dtypes[o])
        body = functools.partial(_mm_fullk_kernel, n_extra=len(rows) + len(tiles), epilogue=epilogue, chunk=chunk)
        scratch, sem = [], ("parallel", "arbitrary")
        assert not emit_w
    else:
        grid = (m // tm, n // tn, kdim // tk)
        in_specs = ([pl.BlockSpec((tm, tk), lambda i, j, k: (i, k)), pl.BlockSpec((tk, tn), lambda i, j, k: (k, j + off))]
                    + [pl.BlockSpec((tm, LANES), lambda i, j, k: (i, 0)) for _ in rows]
                    + [pl.BlockSpec((tm, tn), lambda i, j, k: (i, j)) for _ in tiles])
        out_specs = [pl.BlockSpec((tm, tn), lambda i, j, k: (i, j)) for _ in out_dtypes]
        if emit_w:
            assert m == tm and w.dtype == F32
            out_specs.append(pl.BlockSpec((tk, tn), lambda i, j, k: (k, j)))
            out_shape.append(jax.ShapeDtypeStruct((kdim, n), BF16))
        body = functools.partial(_mm_ksplit_kernel, n_extra=len(rows) + len(tiles), epilogue=epilogue, chunk=chunk,
                                 emit_w=emit_w)
        scratch, sem = [pltpu.VMEM((tm, tn), F32)], ("parallel", "parallel", "arbitrary")
    return pl.pallas_call(body, grid=grid, in_specs=in_specs, out_specs=out_specs, out_shape=out_shape,
                          scratch_shapes=scratch, compiler_params=_params(*sem), name=name)(x, w, *rows, *tiles)


def _lambda(lam_ref, lam_init):
    lp = lam_ref[...]
    s1 = jnp.sum(lp[0:1] * lp[1:2], axis=-1, keepdims=True)
    s2 = jnp.sum(lp[2:3] * lp[3:4], axis=-1, keepdims=True)
    return jnp.exp(s1) - jnp.exp(s2) + lam_init


def _head_rmsnorm(o, g, mult=None):
    y = o * lax.rsqrt(jnp.mean(o * o, axis=-1, keepdims=True) + EPS) * g
    if mult is not None:
        y = y * mult
    return y


def _dot_nt(a, b):
    return lax.dot_general(a, b, (((1,), (1,)), ((), ())), preferred_element_type=F32)


def _log_sigmoid(z):
    return jnp.minimum(z, 0.0) - jnp.log(1.0 + jnp.exp(-jnp.abs(z)))


def _strict_upper(n):
    r = lax.broadcasted_iota(jnp.int32, (n, n), 0)
    c = lax.broadcasted_iota(jnp.int32, (n, n), 1)
    return (r > c).astype(BF16)


def _lane_tile(x, width):
    return jnp.tile(x, (1, width // x.shape[1]))


def _lane_fold(x):
    out = x[:, :LANES]
    for i in range(1, x.shape[1] // LANES):
        out = out + x[:, i * LANES:(i + 1) * LANES]
    return out


def _diff_prompt_kernel(q_ref, k_ref, v_ref, lam_ref, g_ref, o_ref, m_sc, l_sc, acc_sc, s_sc, smax_sc,
                        *, tq, lam_init):
    qi = pl.program_id(1)
    hw = 2 * HEAD_DIM
    m_sc[...] = jnp.full_like(m_sc, NEG)
    l_sc[...] = jnp.zeros_like(l_sc)
    acc_sc[...] = jnp.zeros_like(acc_sc)

    def scores(j, slot, diagonal):
        kb = k_ref[pl.ds(pl.multiple_of(j * tq, tq), tq), :]
        for c in range(2):
            s = _dot_nt(q_ref[:, c * HEAD_DIM:(c + 1) * HEAD_DIM], kb[:, c * HEAD_DIM:(c + 1) * HEAD_DIM])
            if diagonal:
                qc = lax.broadcasted_iota(jnp.int32, (tq, tq), 0) // CHUNK
                kc = lax.broadcasted_iota(jnp.int32, (tq, tq), 1) // CHUNK
                s = jnp.where(kc <= qc, s, NEG)
            s_sc[slot, c] = s
            smax_sc[slot, c] = jnp.broadcast_to(jnp.max(s, axis=1, keepdims=True), (tq, LANES))

    def accumulate(slot, j):
        vb = v_ref[pl.ds(pl.multiple_of(j * tq, tq), tq), :]
        for c in range(2):
            m_old = m_sc[c]
            m_new = jnp.maximum(m_old, smax_sc[slot, c])
            alpha = jnp.exp2(m_old - m_new)
            p = jnp.exp2(s_sc[slot, c] - _lane_tile(m_new, tq))
            l_sc[c] = alpha * l_sc[c] + _lane_fold(p)
            acc_sc[c] = _lane_tile(alpha, hw) * acc_sc[c] + jnp.dot(p.astype(BF16), vb, preferred_element_type=F32)
            m_sc[c] = m_new

    scores(qi, 0, True)

    def body(i, carry):
        scores(2 * i, 1, False)
        accumulate(0, jnp.where(i == 0, qi, 2 * i - 1))
        scores(2 * i + 1, 0, False)
        accumulate(1, 2 * i)
        return carry

    npair = qi // 2
    lax.fori_loop(0, npair, body, 0)
    pending = jnp.where(npair == 0, qi, 2 * npair - 1)

    @pl.when(qi % 2 == 0)
    def _():
        accumulate(0, pending)

    @pl.when(qi % 2 == 1)
    def _():
        scores(qi - 1, 1, False)
        accumulate(0, pending)
        accumulate(1, qi - 1)

    lam = _lambda(lam_ref, lam_init)
    l0 = jnp.sum(l_sc[0], axis=1, keepdims=True)
    l1 = jnp.sum(l_sc[1], axis=1, keepdims=True)
    o = acc_sc[0] / l0 - lam * (acc_sc[1] / l1)
    o_ref[...] = _head_rmsnorm(o, g_ref[...], 1.0 - lam_init).astype(o_ref.dtype)


def _diff_prompt(q, k, v, lam_p, g, lam_init, tq=512):
    t = q.shape[0]
    hw = 2 * HEAD_DIM
    return pl.pallas_call(
        functools.partial(_diff_prompt_kernel, tq=tq, lam_init=lam_init),
        grid=(A_HEADS, t // tq),
        in_specs=[pl.BlockSpec((tq, hw), lambda h, i: (i, h)),
                  pl.BlockSpec((t, hw), lambda h, i: (0, h)),
                  pl.BlockSpec((t, hw), lambda h, i: (0, h)),
                  pl.BlockSpec((4, HEAD_DIM), lambda h, i: (0, 0)),
                  pl.BlockSpec((1, hw), lambda h, i: (0, 0))],
        out_specs=pl.BlockSpec((tq, hw), lambda h, i: (i, h)),
        out_shape=jax.ShapeDtypeStruct((t, A_WIDTH), BF16),
        scratch_shapes=[pltpu.VMEM((2, tq, LANES), F32), pltpu.VMEM((2, tq, LANES), F32),
                        pltpu.VMEM((2, tq, hw), F32), pltpu.VMEM((2, 2, tq, tq), F32),
                        pltpu.VMEM((2, 2, tq, LANES), F32)],
        compiler_params=_params("parallel", "arbitrary"),
        name="diff_prompt",
    )(q, k, v, lam_p, g)


def _cumsum_matrix(n):
    r = lax.broadcasted_iota(jnp.int32, (n, n + LANES), 0)
    c = lax.broadcasted_iota(jnp.int32, (n, n + LANES), 1)
    return ((r > c) | (c >= n)).astype(BF16)


def _stick_prompt_kernel(q_ref, k_ref, v_ref, g_ref, o_ref, carry_sc, acc_sc, *, tq, nh):
    qi = pl.program_id(1)
    cum = _cumsum_matrix(tq)

    def block(kstart, first):
        for hh in range(nh):
            sl = slice(hh * HEAD_DIM, (hh + 1) * HEAD_DIM)
            kb = k_ref[pl.ds(kstart, tq), sl]
            vb = v_ref[pl.ds(kstart, tq), sl]
            z = _dot_nt(q_ref[:, sl], kb)
            lb = _log_sigmoid(z)
            l1 = lb - z
            if first:
                mask = (lax.broadcasted_iota(jnp.int32, (tq, tq), 1)
                        < lax.broadcasted_iota(jnp.int32, (tq, tq), 0))
                l1 = jnp.where(mask, l1, 0.0)
            sums = jnp.dot(l1.astype(BF16), cum, preferred_element_type=F32)
            if first:
                a = jnp.where(mask, jnp.exp(lb + sums[:, :tq]), 0.0)
                carry_sc[hh] = sums[:, tq:]
                acc_sc[hh] = jnp.dot(a.astype(BF16), vb, preferred_element_type=F32)
            else:
                a = jnp.exp(lb + sums[:, :tq] + _lane_tile(carry_sc[hh], tq))
                carry_sc[hh] += sums[:, tq:]
                acc_sc[hh] += jnp.dot(a.astype(BF16), vb, preferred_element_type=F32)

    block(pl.multiple_of(qi * tq, tq), True)

    def cond(st):
        j, cmax = st
        return (j >= 0) & (cmax > UNDERFLOW_LOG)

    def body(st):
        j, _ = st
        block(pl.multiple_of(j * tq, tq), False)
        return j - 1, jnp.max(carry_sc[...])

    lax.while_loop(cond, body, (qi - 1, jnp.max(carry_sc[...])))
    for hh in range(nh):
        sl = slice(hh * HEAD_DIM, (hh + 1) * HEAD_DIM)
        o_ref[:, sl] = _head_rmsnorm(acc_sc[hh], g_ref[...]).astype(o_ref.dtype)


def _stick_prompt(q, k, v, g, tq=256, nh=4):
    t = q.shape[0]
    w = nh * HEAD_DIM
    return pl.pallas_call(
        functools.partial(_stick_prompt_kernel, tq=tq, nh=nh),
        grid=(B_HEADS // nh, t // tq),
        in_specs=[pl.BlockSpec((tq, w), lambda h, i: (i, h)),
                  pl.BlockSpec((t, w), lambda h, i: (0, h)),
                  pl.BlockSpec((t, w), lambda h, i: (0, h)),
                  pl.BlockSpec((1, HEAD_DIM), lambda h, i: (0, 0))],
        out_specs=pl.BlockSpec((tq, w), lambda h, i: (i, h)),
        out_shape=jax.ShapeDtypeStruct((t, B_WIDTH), BF16),
        scratch_shapes=[pltpu.VMEM((nh, tq, LANES), F32), pltpu.VMEM((nh, tq, HEAD_DIM), F32)],
        compiler_params=_params("parallel", "arbitrary"),
        name="stick_prompt",
    )(q, k, v, g)


def _diff_sample_kernel(q_ref, kn_ref, vn_ref, lam_ref, g_ref, kc_hbm, vc_hbm, o_ref, kbuf, vbuf, sem, m_sc, l_sc, acc_sc,
                        *, nb, past, nq, tk, lam_init):
    hw = 2 * HEAD_DIM
    nkb = past // tk
    nsteps = nb * nkb
    qchunk = (past + lax.broadcasted_iota(jnp.int32, (nq, 1), 0)) // CHUNK

    def copies(s, slot):
        b, j = s // nkb, s % nkb
        frames = pl.ds(pl.multiple_of(j * tk, tk), tk)
        ks = [pltpu.make_async_copy(kc_hbm.at[b, frames, hc], kbuf.at[slot, hc], sem.at[slot, hc])
              for hc in range(2 * A_HEADS)]
        vs = [pltpu.make_async_copy(vc_hbm.at[b, frames, h], vbuf.at[slot, h], sem.at[slot, 2 * A_HEADS + h])
              for h in range(A_HEADS)]
        return ks + vs

    def sweep(rows, hc, k, v, kpos, first):
        s = _dot_nt(q_ref[rows, hc * HEAD_DIM:(hc + 1) * HEAD_DIM], k)
        s = jnp.where(kpos // CHUNK <= qchunk, s, NEG)
        m_cur = jnp.broadcast_to(jnp.max(s, axis=1, keepdims=True), (nq, LANES))
        if first:
            m_new = m_cur
            p = jnp.exp2(s - m_new[:, :1])
            l_sc[hc] = jnp.broadcast_to(jnp.sum(p, axis=1, keepdims=True), (nq, LANES))
            acc_sc[hc] = jnp.dot(p.astype(BF16), v, preferred_element_type=F32)
        else:
            m_old = m_sc[hc]
            m_new = jnp.maximum(m_old, m_cur)
            alpha = jnp.exp2(m_old - m_new)
            p = jnp.exp2(s - m_new[:, :1])
            l_sc[hc] = alpha * l_sc[hc] + jnp.sum(p, axis=1, keepdims=True)
            acc_sc[hc] = _lane_tile(alpha, hw) * acc_sc[hc] + jnp.dot(p.astype(BF16), v, preferred_element_type=F32)
        m_sc[hc] = m_new

    for cp in copies(0, 0):
        cp.start()

    def step(s, carry):
        slot = s % 2
        b, j = s // nkb, s % nkb
        rows = pl.ds(pl.multiple_of(b * nq, nq), nq)
        for cp in copies(s, slot):
            cp.wait()

        @pl.when(s + 1 < nsteps)
        def _():
            for cp in copies(s + 1, 1 - slot):
                cp.start()

        @pl.when(j == 0)
        def _():
            kpos = past + lax.broadcasted_iota(jnp.int32, (nq, nq), 1)
            for hc in range(2 * A_HEADS):
                h = hc // 2
                sweep(rows, hc, kn_ref[rows, hc * HEAD_DIM:(hc + 1) * HEAD_DIM], vn_ref[rows, h * hw:(h + 1) * hw],
                      kpos, True)

        kpos = j * tk + lax.broadcasted_iota(jnp.int32, (nq, tk), 1)
        for h in range(A_HEADS):
            v = vbuf[slot, h].astype(BF16)
            for c in range(2):
                sweep(rows, 2 * h + c, kbuf[slot, 2 * h + c].astype(BF16), v, kpos, False)

        @pl.when(j == nkb - 1)
        def _():
            lam = _lambda(lam_ref, lam_init)
            for h in range(A_HEADS):
                o = (acc_sc[2 * h] / _lane_tile(l_sc[2 * h], hw)
                     - lam * (acc_sc[2 * h + 1] / _lane_tile(l_sc[2 * h + 1], hw)))
                o_ref[rows, h * hw:(h + 1) * hw] = _head_rmsnorm(o, g_ref[...], 1.0 - lam_init).astype(o_ref.dtype)

        return carry

    lax.fori_loop(0, nsteps, step, 0)


def _diff_sample(q, kcache, vcache, knew, vnew, lam_p, g, lam_init, nq, tk=512):
    nb, past = kcache.shape[:2]
    hw = 2 * HEAD_DIM
    vmem = pl.BlockSpec(memory_space=pltpu.VMEM)
    hbm = pl.BlockSpec(memory_space=pl.ANY)
    return pl.pallas_call(
        functools.partial(_diff_sample_kernel, nb=nb, past=past, nq=nq, tk=tk, lam_init=lam_init),
        in_specs=[vmem, vmem, vmem, vmem, vmem, hbm, hbm],
        out_specs=vmem,
        out_shape=jax.ShapeDtypeStruct((nb * nq, A_WIDTH), BF16),
        scratch_shapes=[pltpu.VMEM((2, 2 * A_HEADS, tk, HEAD_DIM), F32), pltpu.VMEM((2, A_HEADS, tk, hw), F32),
                        pltpu.SemaphoreType.DMA((2, 3 * A_HEADS)),
                        pltpu.VMEM((2 * A_HEADS, nq, LANES), F32), pltpu.VMEM((2 * A_HEADS, nq, LANES), F32),
                        pltpu.VMEM((2 * A_HEADS, nq, hw), F32)],
        compiler_params=pltpu.CompilerParams(vmem_limit_bytes=VMEM_LIMIT_BYTES),
        name="diff_sample",
    )(q, knew, vnew, lam_p, g, kcache, vcache)


def _stick_sample_kernel(q_ref, kn_ref, vn_ref, g_ref, kc_hbm, vc_hbm, o_ref, kbuf, vbuf, sem, carry_sc, acc_sc,
                         *, nb, nq, tk, nkb):
    cum = _cumsum_matrix(tk)
    mask = lax.broadcasted_iota(jnp.int32, (nq, nq), 1) < lax.broadcasted_iota(jnp.int32, (nq, nq), 0)
    upper = _strict_upper(nq)

    def copies(b, j, slot):
        frames = pl.ds(pl.multiple_of(j * tk, tk), tk)
        ks = [pltpu.make_async_copy(kc_hbm.at[b, frames, h], kbuf.at[slot, h], sem.at[slot, h])
              for h in range(B_HEADS)]
        vs = [pltpu.make_async_copy(vc_hbm.at[b, frames, h], vbuf.at[slot, h], sem.at[slot, B_HEADS + h])
              for h in range(B_HEADS)]
        return ks + vs

    for cp in copies(0, nkb - 1, 0):
        cp.start()

    def stream(b, carry):
        rows = pl.ds(pl.multiple_of(b * nq, nq), nq)
        for h in range(B_HEADS):
            sl = slice(h * HEAD_DIM, (h + 1) * HEAD_DIM)
            z = _dot_nt(q_ref[rows, sl], kn_ref[rows, sl])
            lb = _log_sigmoid(z)
            l1 = jnp.where(mask, lb - z, 0.0)
            tail = jnp.dot(l1.astype(BF16), upper, preferred_element_type=F32)
            a = jnp.where(mask, jnp.exp(lb + tail), 0.0)
            carry_sc[h] = jnp.broadcast_to(jnp.sum(l1, axis=1, keepdims=True), (nq, LANES))
            acc_sc[h] = jnp.dot(a.astype(BF16), vn_ref[rows, sl], preferred_element_type=F32)

        def cond(st):
            j, cmax = st
            return (j >= 0) & (cmax > UNDERFLOW_LOG)

        def body(st):
            j, _ = st
            slot = (nkb - 1 - j) % 2
            for cp in copies(b, j, slot):
                cp.wait()

            @pl.when(j > 0)
            def _():
                for cp in copies(b, j - 1, 1 - slot):
                    cp.start()

            for h in range(B_HEADS):
                z = _dot_nt(q_ref[rows, h * HEAD_DIM:(h + 1) * HEAD_DIM], kbuf[slot, h].astype(BF16))
                lb = _log_sigmoid(z)
                sums = jnp.dot((lb - z).astype(BF16), cum, preferred_element_type=F32)
                a = jnp.exp(lb + sums[:, :tk] + _lane_tile(carry_sc[h], tk))
                carry_sc[h] += sums[:, tk:]
                acc_sc[h] += jnp.dot(a.astype(BF16), vbuf[slot, h].astype(BF16), preferred_element_type=F32)
            return j - 1, jnp.max(carry_sc[...])

        j_end, _ = lax.while_loop(cond, body, (nkb - 1, jnp.max(carry_sc[...])))

        @pl.when(j_end >= 0)
        def _():
            for cp in copies(b, j_end, (nkb - 1 - j_end) % 2):
                cp.wait()

        @pl.when(b + 1 < nb)
        def _():
            for cp in copies(b + 1, nkb - 1, 0):
                cp.start()

        for h in range(B_HEADS):
            o_ref[rows, h * HEAD_DIM:(h + 1) * HEAD_DIM] = _head_rmsnorm(acc_sc[h], g_ref[...]).astype(o_ref.dtype)
        return carry

    lax.fori_loop(0, nb, stream, 0)


def _stick_sample(q, kcache, vcache, knew, vnew, g, nq, tk=256):
    nb, past = kcache.shape[:2]
    vmem = pl.BlockSpec(memory_space=pltpu.VMEM)
    hbm = pl.BlockSpec(memory_space=pl.ANY)
    return pl.pallas_call(
        functools.partial(_stick_sample_kernel, nb=nb, nq=nq, tk=tk, nkb=past // tk),
        in_specs=[vmem, vmem, vmem, vmem, hbm, hbm],
        out_specs=vmem,
        out_shape=jax.ShapeDtypeStruct((nb * nq, B_WIDTH), BF16),
        scratch_shapes=[pltpu.VMEM((2, B_HEADS, tk, HEAD_DIM), F32), pltpu.VMEM((2, B_HEADS, tk, HEAD_DIM), F32),
                        pltpu.SemaphoreType.DMA((2, 2 * B_HEADS)),
                        pltpu.VMEM((B_HEADS, nq, LANES), F32), pltpu.VMEM((B_HEADS, nq, HEAD_DIM), F32)],
        compiler_params=pltpu.CompilerParams(vmem_limit_bytes=VMEM_LIMIT_BYTES),
        name="stick_sample",
    )(q, knew, vnew, g, kcache, vcache)


GROUPS = ((True, SCALE * LOG2E, False), (True, 1.0, True), (False, 1.0, True),
          (False, SCALE, False), (False, 1.0, True), (False, 1.0, True))
assert A_WIDTH == B_WIDTH


def _project_first(x, g_attn, w_in, base, period):
    m = x.shape[0]
    h = _rmsnorm(x, g_attn, BF16)
    tabs = _rope_tables(m, base, period)
    tn = 1024
    ep = functools.partial(_epilogue_inproj_all, tn=tn)
    p32, pbf, wb = _matmul(h, w_in, ep, [F32, BF16], tm=m, tn=tn, tk=1024, rows=tabs, emit_w=True, name="inproj_first")
    cols = lambda a, g: a[:, g * A_WIDTH:(g + 1) * A_WIDTH]
    bf = [cols(pbf, g) for g in range(len(GROUPS))]
    f32 = [cols(p32, g) for g, (_, _, want_f32) in enumerate(GROUPS) if want_f32]
    return bf, f32, wb


def _project_all(x, g_attn, wb_in, base, period):
    m = x.shape[0]
    h = _rmsnorm(x, g_attn, BF16)
    tabs = _rope_tables(m, base, period)
    bf, f32 = [], []
    for g, (rope, scale, want_f32) in enumerate(GROUPS):
        head_major = g == 1
        tn = 512 if head_major else 1024
        ep = functools.partial(_epilogue_inproj, rope=rope, scale=scale, want_f32=want_f32, want_bf16=True,
                               rows_by_head=(tn, A_WIDTH // HEAD_DIM) if head_major else None)
        dts = ([F32] if want_f32 else []) + [BF16]
        res = _matmul(h, wb_in, ep, dts, tm=1024, tn=tn, rows=tabs, n=A_WIDTH, col_block=g,
                      by_head=(0,) if head_major else (), name="inproj")
        bf.append(res[-1])
        if want_f32:
            f32.append(res[0])
    return bf, f32


def _finish(x, oa, ob, w_out, g_mlp, w_up, w_down, g_final, *, first_pass):
    m = x.shape[0]
    o = jnp.concatenate([oa, ob], axis=1)
    if first_pass:
        kw = dict(tm=m, tn=1024, tk=1024, emit_w=True)
        x1, wb_out = _matmul(o, w_out, _epilogue_residual, [F32], tiles=(x,), name="outproj_first", **kw)
        h2 = _rmsnorm(x1, g_mlp, BF16)
        u, wb_up = _matmul(h2, w_up, _epilogue_relu2, [BF16], name="mlp_up_first", **kw)
        x2, wb_down = _matmul(u, w_down, _epilogue_residual, [F32], tiles=(x1,), name="mlp_down_first", **kw)
        wb = (wb_out, wb_up, wb_down)
    else:
        (x1,) = _matmul(o, w_out, _epilogue_residual, [F32], tm=1024, tn=512, tiles=(x,), name="outproj")
        h2 = _rmsnorm(x1, g_mlp, BF16)
        (u,) = _matmul(h2, w_up, _epilogue_relu2, [BF16], tm=1024, tn=1024, name="mlp_up")
        (x2,) = _matmul(u, w_down, _epilogue_residual, [F32], tm=1024, tn=1024, tk=2048, tiles=(x1,), name="mlp_down")
        wb = None
    return _rmsnorm(x2, g_final, F32), wb


def kernel(x_prompt, x_sample, cache_a_k, cache_a_v, cache_b_k, cache_b_v, g_attn, w_in, lambda_q1, lambda_k1,
           lambda_q2, lambda_k2, g_subln, g_sb, w_out, g_mlp, w_up, w_down, g_final):
    depth = w_in.shape[0]
    assert depth == 1
    nb_p, seq, d = x_prompt.shape
    nb_s, dec, _ = x_sample.shape
    past = cache_a_k.shape[2]
    assert nb_p == 1 and d == D_MODEL

    l = 0
    lam_init = 0.8 - 0.6 * math.exp(-0.3 * l)
    lam_p = jnp.stack([lambda_q1[l], lambda_k1[l], lambda_q2[l], lambda_k2[l]]).astype(F32)
    g_sub = g_subln[l].reshape(1, 2 * HEAD_DIM)
    g_b = g_sb[l].reshape(1, HEAD_DIM)

    xs = x_sample.reshape(nb_s * dec, d)
    (aq, ak, av, bq, bk, bv), s32, wb_in = _project_first(xs, g_attn[l], w_in[l], past, dec)
    oa = _diff_sample(aq, cache_a_k[l].reshape(nb_s, past, 2 * A_HEADS, HEAD_DIM), cache_a_v[l], ak, av, lam_p, g_sub,
                      lam_init, dec)
    ob = _stick_sample(bq, cache_b_k[l], cache_b_v[l], bk, bv, g_b, dec)
    y_sample, (wb_out, wb_up, wb_down) = _finish(xs, oa, ob, w_out[l], g_mlp[l], w_up[l], w_down[l], g_final,
                                                 first_pass=True)

    xp = x_prompt.reshape(seq, d)
    (aq, ak, av, bq, bk, bv), p32 = _project_all(xp, g_attn[l], wb_in, 0, seq)
    oa = _diff_prompt(aq, ak, av, lam_p, g_sub, lam_init)
    ob = _stick_prompt(bq, bk, bv, g_b)
    y_prompt, _ = _finish(xp, oa, ob, wb_out, g_mlp[l], wb_up, wb_down, g_final, first_pass=False)

    return (y_prompt.reshape(nb_p, seq, d), y_sample.reshape(nb_s, dec, d),
            p32[0].reshape(1, nb_p, seq, A_HEADS, 2, HEAD_DIM), p32[1].reshape(1, nb_p, seq, A_HEADS, 2 * HEAD_DIM),
            p32[2].reshape(1, nb_p, seq, B_HEADS, HEAD_DIM), p32[3].reshape(1, nb_p, seq, B_HEADS, HEAD_DIM),
            s32[0].reshape(1, nb_s, dec, A_HEADS, 2, HEAD_DIM), s32[1].reshape(1, nb_s, dec, A_HEADS, 2 * HEAD_DIM),
            s32[2].reshape(1, nb_s, dec, B_HEADS, HEAD_DIM), s32[3].reshape(1, nb_s, dec, B_HEADS, HEAD_DIM))
```

```python
import functools
import math

import jax
import jax.numpy as jnp
from jax import lax
from jax.experimental import pallas as pl
from jax.experimental.pallas import tpu as pltpu

D_MODEL = 4096
HEAD_DIM = 128
A_HEADS = 8
B_HEADS = 16
A_WIDTH = A_HEADS * 2 * HEAD_DIM
B_WIDTH = B_HEADS * HEAD_DIM
ROT_DIM = HEAD_DIM // 4
ROPE_THETA = 500000.0
CHUNK = 64
D_FF = 4 * D_MODEL
EPS = 1e-6
NEG = -1e30
SCALE = HEAD_DIM ** -0.5
LOG2E = math.log2(math.e)
LANES = 128
SUBLANES = 8
UNDERFLOW_LOG = -110.0

VMEM_LIMIT_BYTES = 56 * 1024 * 1024

F32 = jnp.float32
BF16 = jnp.bfloat16


def _params(*sem):
    return pltpu.CompilerParams(dimension_semantics=sem, vmem_limit_bytes=VMEM_LIMIT_BYTES)


def _rmsnorm_kernel(x_ref, g_ref, o_ref):
    x = x_ref[...]
    ms = jnp.mean(x * x, axis=-1, keepdims=True)
    o_ref[...] = (x * lax.rsqrt(ms + EPS) * g_ref[...]).astype(o_ref.dtype)


def _rmsnorm(x, g, out_dtype, tm=256):
    m, d = x.shape
    return pl.pallas_call(
        _rmsnorm_kernel,
        grid=(m // tm,),
        in_specs=[pl.BlockSpec((tm, d), lambda i: (i, 0)), pl.BlockSpec((1, d), lambda i: (0, 0))],
        out_specs=pl.BlockSpec((tm, d), lambda i: (i, 0)),
        out_shape=jax.ShapeDtypeStruct((m, d), out_dtype),
        compiler_params=_params("arbitrary"),
        name="rmsnorm",
    )(x, g.reshape(1, d))


def _rope_table_kernel(inv_ref, c_ref, s1_ref, s2_ref, *, tm, base, period):
    half = ROT_DIM // 2
    row = lax.broadcasted_iota(jnp.int32, (tm, HEAD_DIM), 0) + pl.program_id(0) * tm
    lane = lax.broadcasted_iota(jnp.int32, (tm, HEAD_DIM), 1)
    pos = (base + row % period).astype(F32)
    ang = pos * inv_ref[...]
    cos = jnp.cos(ang)
    sin = jnp.sin(ang)
    c_ref[...] = jnp.where(lane < ROT_DIM, cos, 1.0)
    s1_ref[...] = jnp.where(lane < half, -sin, 0.0)
    s2_ref[...] = jnp.where((lane >= half) & (lane < ROT_DIM), sin, 0.0)


def _rope_tables(m, base, period, tm=256):
    half = ROT_DIM // 2
    inv = 1.0 / (ROPE_THETA ** (jnp.arange(0, ROT_DIM, 2, dtype=F32) / ROT_DIM))
    inv_lane = jnp.concatenate([inv, inv, jnp.zeros((HEAD_DIM - ROT_DIM,), F32)]).reshape(1, HEAD_DIM)
    assert inv.shape[0] == half
    spec = pl.BlockSpec((tm, HEAD_DIM), lambda i: (i, 0))
    shp = jax.ShapeDtypeStruct((m, HEAD_DIM), F32)
    return pl.pallas_call(
        functools.partial(_rope_table_kernel, tm=tm, base=base, period=period),
        grid=(m // tm,),
        in_specs=[pl.BlockSpec((1, HEAD_DIM), lambda i: (0, 0))],
        out_specs=[spec, spec, spec],
        out_shape=[shp, shp, shp],
        compiler_params=_params("arbitrary"),
        name="rope_tables",
    )(inv_lane)


def _epilogue_inproj(y, c0, extra, outs, *, rope, scale, want_f32, want_bf16, rows_by_head=None, jb=None):
    c_ref, s1_ref, s2_ref = extra
    half = ROT_DIM // 2
    for hh in range(y.shape[1] // HEAD_DIM):
        yh = y[:, hh * HEAD_DIM:(hh + 1) * HEAD_DIM]
        if rope:
            yh = (yh * c_ref[...]
                  + pltpu.roll(yh, HEAD_DIM - half, 1) * s1_ref[...]
                  + pltpu.roll(yh, half, 1) * s2_ref[...])
        sl = slice(c0 + hh * HEAD_DIM, c0 + (hh + 1) * HEAD_DIM)
        k = 0
        if want_f32:
            if rows_by_head is None:
                outs[k][:, sl] = yh
            else:
                tn, heads = rows_by_head
                head = (jb * tn + c0) // HEAD_DIM + hh
                outs[k][pl.ds(head, yh.shape[0], stride=heads), :] = yh
            k += 1
        if want_bf16:
            outs[k][:, sl] = (yh * scale).astype(BF16)


def _epilogue_inproj_all(y, c0, extra, outs, *, tn):
    g = (pl.program_id(1) * tn + c0) // A_WIDTH
    scale = jnp.where(g == 0, SCALE * LOG2E, jnp.where(g == 3, SCALE, 1.0))

    @pl.when(g < 2)
    def _():
        _epilogue_inproj(y, c0, extra, outs, rope=True, scale=scale, want_f32=True, want_bf16=True)

    @pl.when(g >= 2)
    def _():
        _epilogue_inproj(y, c0, extra, outs, rope=False, scale=scale, want_f32=True, want_bf16=True)


def _epilogue_residual(y, c0, extra, outs, jb=None):
    (r_ref,), (o_ref,) = extra, outs
    sl = slice(c0, c0 + y.shape[1])
    o_ref[:, sl] = r_ref[:, sl] + y


def _epilogue_relu2(y, c0, extra, outs, jb=None):
    (o_ref,) = outs
    u = jnp.maximum(y, 0.0)
    o_ref[:, c0:c0 + y.shape[1]] = (u * u).astype(o_ref.dtype)


def _mm_fullk_kernel(x_ref, w_ref, *rest, n_extra, epilogue, chunk, split_x):
    if split_x:
        x2_ref, rest = rest[0], rest[1:]
    extra, outs = rest[:n_extra], rest[n_extra:]
    k1 = x_ref.shape[1]
    for c0 in range(0, w_ref.shape[1], chunk):
        y = jnp.dot(x_ref[...], w_ref[:k1, c0:c0 + chunk], preferred_element_type=F32)
        if split_x:
            y = y + jnp.dot(x2_ref[...], w_ref[k1:, c0:c0 + chunk], preferred_element_type=F32)
        epilogue(y, c0, extra, outs)


def _mm_fullk_groups_kernel(x_ref, w_ref, *rest, n_extra, groups, chunk):
    extra, outs = rest[:n_extra], rest[n_extra:]
    j = pl.program_id(1)
    k = 0
    for lo, nblk, epilogue, n_out in groups:
        @pl.when((j >= lo) & (j < lo + nblk))
        def _(lo=lo, epilogue=epilogue, mine=outs[k:k + n_out]):
            for c0 in range(0, w_ref.shape[1], chunk):
                y = jnp.dot(x_ref[...], w_ref[:, c0:c0 + chunk], preferred_element_type=F32)
                epilogue(y, c0, extra, mine, jb=j - lo)
        k += n_out


def _matmul_groups(x, w, groups, *, tm, tn, rows=(), col_block=0, chunk=256, name):
    m, kdim = x.shape
    in_specs = ([pl.BlockSpec((tm, kdim), lambda i, j: (i, 0)), pl.BlockSpec((kdim, tn), lambda i, j: (0, j + col_block))]
                + [pl.BlockSpec((tm, LANES), lambda i, j: (i, 0)) for _ in rows])
    out_specs, out_shape, kgroups, lo = [], [], [], 0
    for n, epilogue, out_dtypes, by_head in groups:
        nblk = n // tn
        for o, dt in enumerate(out_dtypes):
            if o in by_head:
                heads = n // HEAD_DIM
                out_specs.append(pl.BlockSpec((tm * heads, HEAD_DIM), lambda i, j: (i, 0)))
                out_shape.append(jax.ShapeDtypeStruct((m * heads, HEAD_DIM), dt))
            else:
                out_specs.append(pl.BlockSpec((tm, tn), lambda i, j, lo=lo, nblk=nblk: (i, jnp.clip(j - lo, 0, nblk - 1))))
                out_shape.append(jax.ShapeDtypeStruct((m, n), dt))
        kgroups.append((lo, nblk, epilogue, len(out_dtypes)))
        lo += nblk
    body = functools.partial(_mm_fullk_groups_kernel, n_extra=len(rows), groups=tuple(kgroups), chunk=chunk)
    return pl.pallas_call(body, grid=(m // tm, lo), in_specs=in_specs, out_specs=out_specs, out_shape=out_shape,
                          compiler_params=_params("parallel", "arbitrary"), name=name)(x, w, *rows)


def _mm_ksplit_kernel(x_ref, w_ref, *rest, n_extra, epilogue, chunk, emit_w):
    extra, outs, acc_ref = rest[:n_extra], rest[n_extra:-1], rest[-1]
    k = pl.program_id(2)

    @pl.when(k == 0)
    def _():
        acc_ref[...] = jnp.zeros_like(acc_ref)

    w = w_ref[...]
    if emit_w:
        outs, wb_ref = outs[:-1], outs[-1]
        w = w.astype(BF16)
        wb_ref[...] = w
    acc_ref[...] += jnp.dot(x_ref[...], w, preferred_element_type=F32)

    @pl.when(k == pl.num_programs(2) - 1)
    def _():
        for c0 in range(0, acc_ref.shape[1], chunk):
            epilogue(acc_ref[:, c0:c0 + chunk], c0, extra, outs)


def _matmul(x, w, epilogue, out_dtypes, *, tm, tn, tk=None, rows=(), tiles=(), n=None, col_block=0, emit_w=False,
            chunk=256, x2=None, name):
    m, kdim = x.shape
    n = w.shape[1] if n is None else n
    off = col_block * (n // tn)
    out_shape = [jax.ShapeDtypeStruct((m, n), dt) for dt in out_dtypes]
    xs = (x,) if x2 is None else (x, x2)
    if tk is None:
        grid = (m // tm, n // tn)
        in_specs = ([pl.BlockSpec((tm, kdim), lambda i, j: (i, 0)), pl.BlockSpec((w.shape[0], tn), lambda i, j: (0, j + off))]
                    + [pl.BlockSpec((tm, a.shape[1]), lambda i, j: (i, 0)) for a in xs[1:]]
                    + [pl.BlockSpec((tm, LANES), lambda i, j: (i, 0)) for _ in rows]
                    + [pl.BlockSpec((tm, tn), lambda i, j: (i, j)) for _ in tiles])
        out_specs = [pl.BlockSpec((tm, tn), lambda i, j: (i, j)) for _ in out_dtypes]
        body = functools.partial(_mm_fullk_kernel, n_extra=len(rows) + len(tiles), epilogue=epilogue, chunk=chunk,
                                 split_x=x2 is not None)
        scratch, sem = [], ("parallel", "arbitrary")
        assert not emit_w
    else:
        assert x2 is None
        grid = (m // tm, n // tn, kdim // tk)
        in_specs = ([pl.BlockSpec((tm, tk), lambda i, j, k: (i, k)), pl.BlockSpec((tk, tn), lambda i, j, k: (k, j + off))]
                    + [pl.BlockSpec((tm, LANES), lambda i, j, k: (i, 0)) for _ in rows]
                    + [pl.BlockSpec((tm, tn), lambda i, j, k: (i, j)) for _ in tiles])
        out_specs = [pl.BlockSpec((tm, tn), lambda i, j, k: (i, j)) for _ in out_dtypes]
        if emit_w:
            assert m == tm and w.dtype == F32
            out_specs.append(pl.BlockSpec((tk, tn), lambda i, j, k: (k, j)))
            out_shape.append(jax.ShapeDtypeStruct((kdim, n), BF16))
        body = functools.partial(_mm_ksplit_kernel, n_extra=len(rows) + len(tiles), epilogue=epilogue, chunk=chunk,
                                 emit_w=emit_w)
        scratch, sem = [pltpu.VMEM((tm, tn), F32)], ("parallel", "parallel", "arbitrary")
    return pl.pallas_call(body, grid=grid, in_specs=in_specs, out_specs=out_specs, out_shape=out_shape,
                          scratch_shapes=scratch, compiler_params=_params(*sem), name=name)(x, w, *xs[1:], *rows, *tiles)


def _lambda(lam_ref, lam_init):
    lp = lam_ref[...]
    s1 = jnp.sum(lp[0:1] * lp[1:2], axis=-1, keepdims=True)
    s2 = jnp.sum(lp[2:3] * lp[3:4], axis=-1, keepdims=True)
    return jnp.exp(s1) - jnp.exp(s2) + lam_init


def _head_rmsnorm(o, g, mult=None):
    y = o * lax.rsqrt(jnp.mean(o * o, axis=-1, keepdims=True) + EPS) * g
    if mult is not None:
        y = y * mult
    return y


def _dot_nt(a, b):
    return lax.dot_general(a, b, (((1,), (1,)), ((), ())), preferred_element_type=F32)


def _log_sigmoid(z):
    return jnp.minimum(z, 0.0) - jnp.log(1.0 + jnp.exp(-jnp.abs(z)))


def _strict_upper(n):
    r = lax.broadcasted_iota(jnp.int32, (n, n), 0)
    c = lax.broadcasted_iota(jnp.int32, (n, n), 1)
    return (r > c).astype(BF16)


def _lane_tile(x, width):
    return jnp.tile(x, (1, width // x.shape[1]))


def _lane_fold(x):
    out = x[:, :LANES]
    for i in range(1, x.shape[1] // LANES):
        out = out + x[:, i * LANES:(i + 1) * LANES]
    return out


def _diff_prompt_kernel(q_ref, k_ref, v_ref, lam_ref, g_ref, o_ref, m_sc, l_sc, acc_sc, s_sc, smax_sc,
                        *, tq, lam_init):
    qi = pl.program_id(1)
    hw = 2 * HEAD_DIM
    m_sc[...] = jnp.full_like(m_sc, NEG)
    l_sc[...] = jnp.zeros_like(l_sc)
    acc_sc[...] = jnp.zeros_like(acc_sc)

    def scores(j, slot, diagonal):
        kb = k_ref[pl.ds(pl.multiple_of(j * tq, tq), tq), :]
        for c in range(2):
            s = _dot_nt(q_ref[:, c * HEAD_DIM:(c + 1) * HEAD_DIM], kb[:, c * HEAD_DIM:(c + 1) * HEAD_DIM])
            if diagonal:
                qc = lax.broadcasted_iota(jnp.int32, (tq, tq), 0) // CHUNK
                kc = lax.broadcasted_iota(jnp.int32, (tq, tq), 1) // CHUNK
                s = jnp.where(kc <= qc, s, NEG)
            s_sc[slot, c] = s
            smax_sc[slot, c] = jnp.broadcast_to(jnp.max(s, axis=1, keepdims=True), (tq, LANES))

    def accumulate(slot, j):
        vb = v_ref[pl.ds(pl.multiple_of(j * tq, tq), tq), :]
        for c in range(2):
            m_old = m_sc[c]
            m_new = jnp.maximum(m_old, smax_sc[slot, c])
            alpha = jnp.exp2(m_old - m_new)
            p = jnp.exp2(s_sc[slot, c] - _lane_tile(m_new, tq))
            l_sc[c] = alpha * l_sc[c] + _lane_fold(p)
            acc_sc[c] = _lane_tile(alpha, hw) * acc_sc[c] + jnp.dot(p.astype(BF16), vb, preferred_element_type=F32)
            m_sc[c] = m_new

    scores(qi, 0, True)

    def body(i, carry):
        scores(2 * i, 1, False)
        accumulate(0, jnp.where(i == 0, qi, 2 * i - 1))
        scores(2 * i + 1, 0, False)
        accumulate(1, 2 * i)
        return carry

    npair = qi // 2
    lax.fori_loop(0, npair, body, 0)
    pending = jnp.where(npair == 0, qi, 2 * npair - 1)

    @pl.when(qi % 2 == 0)
    def _():
        accumulate(0, pending)

    @pl.when(qi % 2 == 1)
    def _():
        scores(qi - 1, 1, False)
        accumulate(0, pending)
        accumulate(1, qi - 1)

    lam = _lambda(lam_ref, lam_init)
    l0 = jnp.sum(l_sc[0], axis=1, keepdims=True)
    l1 = jnp.sum(l_sc[1], axis=1, keepdims=True)
    o = acc_sc[0] / l0 - lam * (acc_sc[1] / l1)
    o_ref[...] = _head_rmsnorm(o, g_ref[...], 1.0 - lam_init).astype(o_ref.dtype)


def _diff_prompt(q, k, v, lam_p, g, lam_init, tq=512):
    t = q.shape[0]
    hw = 2 * HEAD_DIM
    return pl.pallas_call(
        functools.partial(_diff_prompt_kernel, tq=tq, lam_init=lam_init),
        grid=(A_HEADS, t // tq),
        in_specs=[pl.BlockSpec((tq, hw), lambda h, i: (i, h)),
                  pl.BlockSpec((t, hw), lambda h, i: (0, h)),
                  pl.BlockSpec((t, hw), lambda h, i: (0, h)),
                  pl.BlockSpec((4, HEAD_DIM), lambda h, i: (0, 0)),
                  pl.BlockSpec((1, hw), lambda h, i: (0, 0))],
        out_specs=pl.BlockSpec((tq, hw), lambda h, i: (i, h)),
        out_shape=jax.ShapeDtypeStruct((t, A_WIDTH), BF16),
        scratch_shapes=[pltpu.VMEM((2, tq, LANES), F32), pltpu.VMEM((2, tq, LANES), F32),
                        pltpu.VMEM((2, tq, hw), F32), pltpu.VMEM((2, 2, tq, tq), F32),
                        pltpu.VMEM((2, 2, tq, LANES), F32)],
        compiler_params=_params("parallel", "arbitrary"),
        name="diff_prompt",
    )(q, k, v, lam_p, g)


def _cumsum_matrix(n):
    r = lax.broadcasted_iota(jnp.int32, (n, n + LANES), 0)
    c = lax.broadcasted_iota(jnp.int32, (n, n + LANES), 1)
    return ((r > c) | (c >= n)).astype(BF16)


def _stick_prompt_kernel(q_ref, k_ref, v_ref, g_ref, o_ref, carry_sc, acc_sc, *, tq, nh):
    qi = pl.program_id(1)
    cum = _cumsum_matrix(tq)

    def block(kstart, first):
        for hh in range(nh):
            sl = slice(hh * HEAD_DIM, (hh + 1) * HEAD_DIM)
            kb = k_ref[pl.ds(kstart, tq), sl]
            vb = v_ref[pl.ds(kstart, tq), sl]
            z = _dot_nt(q_ref[:, sl], kb)
            lb = _log_sigmoid(z)
            l1 = lb - z
            if first:
                mask = (lax.broadcasted_iota(jnp.int32, (tq, tq), 1)
                        < lax.broadcasted_iota(jnp.int32, (tq, tq), 0))
                l1 = jnp.where(mask, l1, 0.0)
            sums = jnp.dot(l1.astype(BF16), cum, preferred_element_type=F32)
            if first:
                a = jnp.where(mask, jnp.exp(lb + sums[:, :tq]), 0.0)
                carry_sc[hh] = sums[:, tq:]
                acc_sc[hh] = jnp.dot(a.astype(BF16), vb, preferred_element_type=F32)
            else:
                a = jnp.exp(lb + sums[:, :tq] + _lane_tile(carry_sc[hh], tq))
                carry_sc[hh] += sums[:, tq:]
                acc_sc[hh] += jnp.dot(a.astype(BF16), vb, preferred_element_type=F32)

    @pl.when(qi == 0)
    def _():
        block(0, True)

    @pl.when(qi > 0)
    def _():
        block(pl.multiple_of(qi * tq, tq), True)
        block(pl.multiple_of((qi - 1) * tq, tq), False)

    def cond(st):
        j, cmax = st
        return (j >= 0) & (cmax > UNDERFLOW_LOG)

    def body(st):
        j, _ = st
        block(pl.multiple_of(j * tq, tq), False)
        return j - 1, jnp.max(carry_sc[...])

    lax.while_loop(cond, body, (qi - 2, jnp.max(carry_sc[...])))
    for hh in range(nh):
        sl = slice(hh * HEAD_DIM, (hh + 1) * HEAD_DIM)
        o_ref[:, sl] = _head_rmsnorm(acc_sc[hh], g_ref[...]).astype(o_ref.dtype)


def _stick_prompt(q, k, v, g, tq=256, nh=4):
    t = q.shape[0]
    w = nh * HEAD_DIM
    return pl.pallas_call(
        functools.partial(_stick_prompt_kernel, tq=tq, nh=nh),
        grid=(B_HEADS // nh, t // tq),
        in_specs=[pl.BlockSpec((tq, w), lambda h, i: (i, h)),
                  pl.BlockSpec((t, w), lambda h, i: (0, h)),
                  pl.BlockSpec((t, w), lambda h, i: (0, h)),
                  pl.BlockSpec((1, HEAD_DIM), lambda h, i: (0, 0))],
        out_specs=pl.BlockSpec((tq, w), lambda h, i: (i, h)),
        out_shape=jax.ShapeDtypeStruct((t, B_WIDTH), BF16),
        scratch_shapes=[pltpu.VMEM((nh, tq, LANES), F32), pltpu.VMEM((nh, tq, HEAD_DIM), F32)],
        compiler_params=_params("parallel", "arbitrary"),
        name="stick_prompt",
    )(q, k, v, g)


def _diff_sample_kernel(q_ref, kn_ref, vn_ref, lam_ref, g_ref, kc_hbm, vc_hbm, o_ref, kbuf, vbuf, sem, m_sc, l_sc, acc_sc,
                        *, nb, past, nq, tk, lam_init):
    hw = 2 * HEAD_DIM
    nkb = past // tk
    nsteps = nb * nkb
    qchunk = (past + lax.broadcasted_iota(jnp.int32, (nq, 1), 0)) // CHUNK

    def copies(s, slot):
        b, j = s // nkb, s % nkb
        frames = pl.ds(pl.multiple_of(j * tk, tk), tk)
        ks = [pltpu.make_async_copy(kc_hbm.at[b, frames, hc], kbuf.at[slot, hc], sem.at[slot, hc])
              for hc in range(2 * A_HEADS)]
        vs = [pltpu.make_async_copy(vc_hbm.at[b, frames, h], vbuf.at[slot, h], sem.at[slot, 2 * A_HEADS + h])
              for h in range(A_HEADS)]
        return ks + vs

    def sweep(rows, hc, k, v, kpos, first):
        s = _dot_nt(q_ref[rows, hc * HEAD_DIM:(hc + 1) * HEAD_DIM], k)
        s = jnp.where(kpos // CHUNK <= qchunk, s, NEG)
        m_cur = jnp.broadcast_to(jnp.max(s, axis=1, keepdims=True), (nq, LANES))
        if first:
            m_new = m_cur
            p = jnp.exp2(s - m_new[:, :1])
            l_sc[hc] = jnp.broadcast_to(jnp.sum(p, axis=1, keepdims=True), (nq, LANES))
            acc_sc[hc] = jnp.dot(p.astype(BF16), v, preferred_element_type=F32)
        else:
            m_old = m_sc[hc]
            m_new = jnp.maximum(m_old, m_cur)
            alpha = jnp.exp2(m_old - m_new)
            p = jnp.exp2(s - m_new[:, :1])
            l_sc[hc] = alpha * l_sc[hc] + jnp.sum(p, axis=1, keepdims=True)
            acc_sc[hc] = _lane_tile(alpha, hw) * acc_sc[hc] + jnp.dot(p.astype(BF16), v, preferred_element_type=F32)
        m_sc[hc] = m_new

    for cp in copies(0, 0):
        cp.start()

    def step(s, carry):
        slot = s % 2
        b, j = s // nkb, s % nkb
        rows = pl.ds(pl.multiple_of(b * nq, nq), nq)
        for cp in copies(s, slot):
            cp.wait()

        @pl.when(s + 1 < nsteps)
        def _():
            for cp in copies(s + 1, 1 - slot):
                cp.start()

        @pl.when(j == 0)
        def _():
            kpos = past + lax.broadcasted_iota(jnp.int32, (nq, nq), 1)
            for hc in range(2 * A_HEADS):
                h = hc // 2
                sweep(rows, hc, kn_ref[rows, hc * HEAD_DIM:(hc + 1) * HEAD_DIM], vn_ref[rows, h * hw:(h + 1) * hw],
                      kpos, True)

        kpos = j * tk + lax.broadcasted_iota(jnp.int32, (nq, tk), 1)
        for h in range(A_HEADS):
            v = vbuf[slot, h].astype(BF16)
            for c in range(2):
                sweep(rows, 2 * h + c, kbuf[slot, 2 * h + c].astype(BF16), v, kpos, False)

        @pl.when(j == nkb - 1)
        def _():
            lam = _lambda(lam_ref, lam_init)
            for h in range(A_HEADS):
                o = (acc_sc[2 * h] / _lane_tile(l_sc[2 * h], hw)
                     - lam * (acc_sc[2 * h + 1] / _lane_tile(l_sc[2 * h + 1], hw)))
                o_ref[rows, h * hw:(h + 1) * hw] = _head_rmsnorm(o, g_ref[...], 1.0 - lam_init).astype(o_ref.dtype)

        return carry

    lax.fori_loop(0, nsteps, step, 0)


def _diff_sample(q, kcache, vcache, knew, vnew, lam_p, g, lam_init, nq, tk=512):
    nb, past = kcache.shape[:2]
    hw = 2 * HEAD_DIM
    vmem = pl.BlockSpec(memory_space=pltpu.VMEM)
    hbm = pl.BlockSpec(memory_space=pl.ANY)
    return pl.pallas_call(
        functools.partial(_diff_sample_kernel, nb=nb, past=past, nq=nq, tk=tk, lam_init=lam_init),
        in_specs=[vmem, vmem, vmem, vmem, vmem, hbm, hbm],
        out_specs=vmem,
        out_shape=jax.ShapeDtypeStruct((nb * nq, A_WIDTH), BF16),
        scratch_shapes=[pltpu.VMEM((2, 2 * A_HEADS, tk, HEAD_DIM), F32), pltpu.VMEM((2, A_HEADS, tk, hw), F32),
                        pltpu.SemaphoreType.DMA((2, 3 * A_HEADS)),
                        pltpu.VMEM((2 * A_HEADS, nq, LANES), F32), pltpu.VMEM((2 * A_HEADS, nq, LANES), F32),
                        pltpu.VMEM((2 * A_HEADS, nq, hw), F32)],
        compiler_params=pltpu.CompilerParams(vmem_limit_bytes=VMEM_LIMIT_BYTES),
        name="diff_sample",
    )(q, knew, vnew, lam_p, g, kcache, vcache)


def _stick_sample_kernel(q_ref, kn_ref, vn_ref, g_ref, kc_hbm, vc_hbm, o_ref, kbuf, vbuf, sem, carry_sc, acc_sc,
                         *, nb, nq, tk, nkb):
    cum = _cumsum_matrix(tk)
    mask = lax.broadcasted_iota(jnp.int32, (nq, nq), 1) < lax.broadcasted_iota(jnp.int32, (nq, nq), 0)
    upper = _strict_upper(nq)

    def copies(b, j, slot):
        frames = pl.ds(pl.multiple_of(j * tk, tk), tk)
        ks = [pltpu.make_async_copy(kc_hbm.at[b, frames, h], kbuf.at[slot, h], sem.at[slot, h])
              for h in range(B_HEADS)]
        vs = [pltpu.make_async_copy(vc_hbm.at[b, frames, h], vbuf.at[slot, h], sem.at[slot, B_HEADS + h])
              for h in range(B_HEADS)]
        return ks + vs

    for cp in copies(0, nkb - 1, 0):
        cp.start()

    def stream(b, carry):
        rows = pl.ds(pl.multiple_of(b * nq, nq), nq)
        for h in range(B_HEADS):
            sl = slice(h * HEAD_DIM, (h + 1) * HEAD_DIM)
            z = _dot_nt(q_ref[rows, sl], kn_ref[rows, sl])
            lb = _log_sigmoid(z)
            l1 = jnp.where(mask, lb - z, 0.0)
            tail = jnp.dot(l1.astype(BF16), upper, preferred_element_type=F32)
            a = jnp.where(mask, jnp.exp(lb + tail), 0.0)
            carry_sc[h] = jnp.broadcast_to(jnp.sum(l1, axis=1, keepdims=True), (nq, LANES))
            acc_sc[h] = jnp.dot(a.astype(BF16), vn_ref[rows, sl], preferred_element_type=F32)

        def cond(st):
            j, cmax = st
            return (j >= 0) & (cmax > UNDERFLOW_LOG)

        def body(st):
            j, _ = st
            slot = (nkb - 1 - j) % 2
            for cp in copies(b, j, slot):
                cp.wait()

            @pl.when(j > 0)
            def _():
                for cp in copies(b, j - 1, 1 - slot):
                    cp.start()

            for h in range(B_HEADS):
                z = _dot_nt(q_ref[rows, h * HEAD_DIM:(h + 1) * HEAD_DIM], kbuf[slot, h].astype(BF16))
                lb = _log_sigmoid(z)
                sums = jnp.dot((lb - z).astype(BF16), cum, preferred_element_type=F32)
                a = jnp.exp(lb + sums[:, :tk] + _lane_tile(carry_sc[h], tk))
                carry_sc[h] += sums[:, tk:]
                acc_sc[h] += jnp.dot(a.astype(BF16), vbuf[slot, h].astype(BF16), preferred_element_type=F32)
            return j - 1, jnp.max(carry_sc[...])

        j_end, _ = lax.while_loop(cond, body, (nkb - 1, jnp.max(carry_sc[...])))

        @pl.when(j_end >= 0)
        def _():
            for cp in copies(b, j_end, (nkb - 1 - j_end) % 2):
                cp.wait()

        @pl.when(b + 1 < nb)
        def _():
            for cp in copies(b + 1, nkb - 1, 0):
                cp.start()

        for h in range(B_HEADS):
            o_ref[rows, h * HEAD_DIM:(h + 1) * HEAD_DIM] = _head_rmsnorm(acc_sc[h], g_ref[...]).astype(o_ref.dtype)
        return carry

    lax.fori_loop(0, nb, stream, 0)


def _stick_sample(q, kcache, vcache, knew, vnew, g, nq, tk=256):
    nb, past = kcache.shape[:2]
    vmem = pl.BlockSpec(memory_space=pltpu.VMEM)
    hbm = pl.BlockSpec(memory_space=pl.ANY)
    return pl.pallas_call(
        functools.partial(_stick_sample_kernel, nb=nb, nq=nq, tk=tk, nkb=past // tk),
        in_specs=[vmem, vmem, vmem, vmem, hbm, hbm],
        out_specs=vmem,
        out_shape=jax.ShapeDtypeStruct((nb * nq, B_WIDTH), BF16),
        scratch_shapes=[pltpu.VMEM((2, B_HEADS, tk, HEAD_DIM), F32), pltpu.VMEM((2, B_HEADS, tk, HEAD_DIM), F32),
                        pltpu.SemaphoreType.DMA((2, 2 * B_HEADS)),
                        pltpu.VMEM((B_HEADS, nq, LANES), F32), pltpu.VMEM((B_HEADS, nq, HEAD_DIM), F32)],
        compiler_params=pltpu.CompilerParams(vmem_limit_bytes=VMEM_LIMIT_BYTES),
        name="stick_sample",
    )(q, knew, vnew, g, kcache, vcache)


GROUPS = ((True, SCALE * LOG2E, False), (True, 1.0, True), (False, 1.0, True),
          (False, SCALE, False), (False, 1.0, True), (False, 1.0, True))
assert A_WIDTH == B_WIDTH


def _project_first(x, g_attn, w_in, base, period):
    m = x.shape[0]
    h = _rmsnorm(x, g_attn, BF16)
    tabs = _rope_tables(m, base, period)
    tn = 1024
    ep = functools.partial(_epilogue_inproj_all, tn=tn)
    p32, pbf, wb = _matmul(h, w_in, ep, [F32, BF16], tm=m, tn=tn, tk=1024, rows=tabs, emit_w=True, name="inproj_first")
    cols = lambda a, g: a[:, g * A_WIDTH:(g + 1) * A_WIDTH]
    bf = [cols(pbf, g) for g in range(len(GROUPS))]
    f32 = [cols(p32, g) for g, (_, _, want_f32) in enumerate(GROUPS) if want_f32]
    return bf, f32, wb


def _project_all(x, g_attn, wb_in, base, period):
    m = x.shape[0]
    h = _rmsnorm(x, g_attn, BF16)
    tabs = _rope_tables(m, base, period)
    tn = 512
    bf, f32 = [], []
    for pair in range(len(GROUPS) // 2):
        groups = []
        for g in (2 * pair, 2 * pair + 1):
            rope, scale, want_f32 = GROUPS[g]
            head_major = g == 1
            ep = functools.partial(_epilogue_inproj, rope=rope, scale=scale, want_f32=want_f32, want_bf16=True,
                                   rows_by_head=(tn, A_WIDTH // HEAD_DIM) if head_major else None)
            groups.append((A_WIDTH, ep, ([F32] if want_f32 else []) + [BF16], (0,) if head_major else ()))
        res = list(_matmul_groups(h, wb_in, groups, tm=1024, tn=tn, rows=tabs, col_block=2 * pair * (A_WIDTH // tn),
                                  name="inproj"))
        for g in (2 * pair, 2 * pair + 1):
            if GROUPS[g][2]:
                f32.append(res.pop(0))
            bf.append(res.pop(0))
    return bf, f32


def _finish(x, oa, ob, w_out, g_mlp, w_up, w_down, g_final, *, first_pass):
    m = x.shape[0]
    if first_pass:
        o = jnp.concatenate([oa, ob], axis=1)
        kw = dict(tm=m, tn=1024, tk=1024, emit_w=True)
        x1, wb_out = _matmul(o, w_out, _epilogue_residual, [F32], tiles=(x,), name="outproj_first", **kw)
        h2 = _rmsnorm(x1, g_mlp, BF16)
        u, wb_up = _matmul(h2, w_up, _epilogue_relu2, [BF16], name="mlp_up_first", **kw)
        x2, wb_down = _matmul(u, w_down, _epilogue_residual, [F32], tiles=(x1,), name="mlp_down_first", **kw)
        wb = (wb_out, wb_up, wb_down)
    else:
        (x1,) = _matmul(oa, w_out, _epilogue_residual, [F32], tm=1024, tn=512, tiles=(x,), x2=ob, name="outproj")
        h2 = _rmsnorm(x1, g_mlp, BF16)
        (u,) = _matmul(h2, w_up, _epilogue_relu2, [BF16], tm=1024, tn=1024, name="mlp_up")
        (x2,) = _matmul(u, w_down, _epilogue_residual, [F32], tm=1024, tn=1024, tk=2048, tiles=(x1,), name="mlp_down")
        wb = None
    return _rmsnorm(x2, g_final, F32), wb


def kernel(x_prompt, x_sample, cache_a_k, cache_a_v, cache_b_k, cache_b_v, g_attn, w_in, lambda_q1, lambda_k1,
           lambda_q2, lambda_k2, g_subln, g_sb, w_out, g_mlp, w_up, w_down, g_final):
    depth = w_in.shape[0]
    assert depth == 1
    nb_p, seq, d = x_prompt.shape
    nb_s, dec, _ = x_sample.shape
    past = cache_a_k.shape[2]
    assert nb_p == 1 and d == D_MODEL

    l = 0
    lam_init = 0.8 - 0.6 * math.exp(-0.3 * l)
    lam_p = jnp.stack([lambda_q1[l], lambda_k1[l], lambda_q2[l], lambda_k2[l]]).astype(F32)
    g_sub = g_subln[l].reshape(1, 2 * HEAD_DIM)
    g_b = g_sb[l].reshape(1, HEAD_DIM)

    xs = x_sample.reshape(nb_s * dec, d)
    (aq, ak, av, bq, bk, bv), s32, wb_in = _project_first(xs, g_attn[l], w_in[l], past, dec)
    oa = _diff_sample(aq, cache_a_k[l].reshape(nb_s, past, 2 * A_HEADS, HEAD_DIM), cache_a_v[l], ak, av, lam_p, g_sub,
                      lam_init, dec)
    ob = _stick_sample(bq, cache_b_k[l], cache_b_v[l], bk, bv, g_b, dec)
    y_sample, (wb_out, wb_up, wb_down) = _finish(xs, oa, ob, w_out[l], g_mlp[l], w_up[l], w_down[l], g_final,
                                                 first_pass=True)

    xp = x_prompt.reshape(seq, d)
    (aq, ak, av, bq, bk, bv), p32 = _project_all(xp, g_attn[l], wb_in, 0, seq)
    oa = _diff_prompt(aq, ak, av, lam_p, g_sub, lam_init)
    ob = _stick_prompt(bq, bk, bv, g_b)
    y_prompt, _ = _finish(xp, oa, ob, wb_out, g_mlp[l], wb_up, wb_down, g_final, first_pass=False)

    return (y_prompt.reshape(nb_p, seq, d), y_sample.reshape(nb_s, dec, d),
            p32[0].reshape(1, nb_p, seq, A_HEADS, 2, HEAD_DIM), p32[1].reshape(1, nb_p, seq, A_HEADS, 2 * HEAD_DIM),
            p32[2].reshape(1, nb_p, seq, B_HEADS, HEAD_DIM), p32[3].reshape(1, nb_p, seq, B_HEADS, HEAD_DIM),
            s32[0].reshape(1, nb_s, dec, A_HEADS, 2, HEAD_DIM), s32[1].reshape(1, nb_s, dec, A_HEADS, 2 * HEAD_DIM),
            s32[2].reshape(1, nb_s, dec, B_HEADS, HEAD_DIM), s32[3].reshape(1, nb_s, dec, B_HEADS, HEAD_DIM))
```
